```python
import math, functools
import jax, jax.numpy as jnp
from jax import lax
import numpy as np

D_MODEL = 4096
BATCH = 2
SEQ = 4096
DEPTH = 1
DEC_BATCH = 128
DEC_SEQ = 8
PAST_LEN = 2048
PAGE_SIZE = 128

ATT_HEAD_DIM = 128
ATT_V_DIM = 2 * ATT_HEAD_DIM
ATT_HEADS = (D_MODEL // 2) // ATT_V_DIM
ATT_WIDTH = ATT_HEADS * ATT_V_DIM
ATT_SCALE = ATT_HEAD_DIM ** -0.5
Q_BLOCK = 128
NEG_INF = -1e30
RMS_EPS = 1e-5
RWKV_HEAD = 64
RWKV_WIDTH = D_MODEL // 2
RWKV_HEADS = RWKV_WIDTH // RWKV_HEAD
DECAY_LORA = max(32, int(round(1.8 * D_MODEL ** 0.5 / 32)) * 32)
ICLR_LORA = max(32, int(round(1.8 * D_MODEL ** 0.5 / 32)) * 32)
GATE_LORA = max(32, int(round(0.6 * D_MODEL ** 0.8 / 32)) * 32)
GN_EPS = 64e-5
D_FF = ((8 * D_MODEL // 3 + 255) // 256) * 256
LN_EPS = 1e-5
ALPHA = (2 * DEPTH) ** 0.25
BETA = (8 * DEPTH) ** -0.25
SHIFT_W = 3 * RWKV_WIDTH + DECAY_LORA + ICLR_LORA + GATE_LORA
Q_OFF = 0
K_OFF = Q_OFF + 2 * ATT_HEADS * ATT_HEAD_DIM
V_OFF = K_OFF + 2 * ATT_HEADS * ATT_HEAD_DIM
SH_OFF = V_OFF + ATT_WIDTH
G_OFF = SH_OFF + SHIFT_W
IN_COLS = G_OFF + 2 * D_MODEL

kernel_name = "diffattn_rwkv7_gated_macaron_deepnorm_step"


def layer_norm(x, g, b):
    xf = x.astype(jnp.float32)
    mu = jnp.mean(xf, -1, keepdims=True)
    var = jnp.mean(jnp.square(xf - mu), -1, keepdims=True)
    return ((xf - mu) * lax.rsqrt(var + LN_EPS) * g + b).astype(x.dtype)


def rms_norm(x, g):
    xf = x.astype(jnp.float32)
    return (xf * lax.rsqrt(jnp.mean(xf * xf, -1, keepdims=True) + RMS_EPS) * g).astype(x.dtype)


def swiglu(x, w_in, w_out):
    gate, up = jnp.split(x @ w_in, 2, axis=-1)
    return (jax.nn.silu(gate) * up) @ w_out


def diff_attend(q, segments, lam):
    d = ATT_HEAD_DIM

    def softmax_map(lo):
        s = jnp.concatenate(
            [jnp.where(mask, jnp.einsum('bqhd,bkhd->bhqk', q[..., lo:lo + d], k[..., lo:lo + d]).astype(jnp.float32) * ATT_SCALE, NEG_INF)
             for k, _, mask in segments], axis=-1)
        return jax.nn.softmax(s, axis=-1)

    p = (softmax_map(0) - lam * softmax_map(d)).astype(q.dtype)
    sizes = [k.shape[1] for k, _, _ in segments]
    parts = jnp.split(p, np.cumsum(sizes)[:-1].tolist(), axis=-1)
    outs = [jnp.einsum('bhqk,bkhe->bqhe', pi, v) for pi, (_, v, _) in zip(parts, segments)]
    return sum(outs[1:], outs[0])


def prompt_attention(q, k, v, lam):
    B, S = q.shape[:2]
    nb = S // Q_BLOCK
    q_blocks = jnp.swapaxes(q.reshape(B, nb, Q_BLOCK, ATT_HEADS, 2 * ATT_HEAD_DIM), 0, 1)
    k_pos = jnp.arange(S)

    def one_block(args):
        i, qb = args
        q_pos = i * Q_BLOCK + jnp.arange(Q_BLOCK)
        return diff_attend(qb, ((k, v, k_pos[None, :] <= q_pos[:, None]),), lam)

    o = lax.map(one_block, (jnp.arange(nb), q_blocks))
    return jnp.swapaxes(o, 0, 1).reshape(B, S, ATT_HEADS, ATT_V_DIM)


def sample_attention(q, k, v, lam, cache_k, cache_v, page_table):
    Bd, T = q.shape[:2]
    past_len = page_table.shape[1] * cache_k.shape[1]
    past_k = cache_k[page_table].reshape(Bd, past_len, ATT_HEADS, 2 * ATT_HEAD_DIM)
    past_v = cache_v[page_table].reshape(Bd, past_len, ATT_HEADS, ATT_V_DIM)
    see_past = jnp.ones((T, past_len), dtype=bool)
    causal = jnp.tril(jnp.ones((T, T), dtype=bool))
    return diff_attend(q, ((past_k, past_v, see_past), (k, v, causal)), lam)


def rwkv7_scan(s0, r, w, k, v, a, b):
    def step(s, inp):
        rt, wt, kt, vt, at, bt = inp
        sa = jnp.einsum('bhvk,bhk->bhv', s, at)
        s = s * wt[:, :, None, :] + sa[..., None] * bt[:, :, None, :] + vt[..., None] * kt[:, :, None, :]
        return s, jnp.einsum('bhvk,bhk->bhv', s, rt)

    seq = tuple(jnp.swapaxes(t.astype(jnp.float32), 0, 1) for t in (r, w, k, v, a, b))
    s_final, y = lax.scan(step, s0.astype(jnp.float32), seq)
    return s_final, jnp.swapaxes(y, 0, 1)


def token_mixing(x, lp, lam_init, attend, rwkv_s0, shift0):
    B, T, _ = x.shape
    f32 = jnp.float32
    z = x @ lp['w_in']

    q = z[..., Q_OFF:K_OFF].reshape(B, T, ATT_HEADS, 2 * ATT_HEAD_DIM)
    k = z[..., K_OFF:V_OFF].reshape(B, T, ATT_HEADS, 2 * ATT_HEAD_DIM)
    v = z[..., V_OFF:SH_OFF].reshape(B, T, ATT_HEADS, ATT_V_DIM)
    lam = (jnp.exp(jnp.sum(lp['lambda_q1'].astype(f32) * lp['lambda_k1'].astype(f32)))
           - jnp.exp(jnp.sum(lp['lambda_q2'].astype(f32) * lp['lambda_k2'].astype(f32))) + lam_init)
    o = attend(q, k, v, lam)
    o = rms_norm(o, lp['subln_g']) * (1.0 - lam_init)
    y_a = o.reshape(B, T, ATT_WIDTH) @ lp['p_a']

    zs = z[..., SH_OFF:G_OFF]
    z_prev = jnp.concatenate([shift0[:, None, :].astype(zs.dtype), zs[:, :-1]], axis=1)
    zm = zs + (z_prev - zs) * lp['mu_shift']
    r, kr, vr, wd, ad, gd = jnp.split(zm, [RWKV_WIDTH, 2 * RWKV_WIDTH, 3 * RWKV_WIDTH,
                                           3 * RWKV_WIDTH + DECAY_LORA,
                                           3 * RWKV_WIDTH + DECAY_LORA + ICLR_LORA], axis=-1)
    w_log = -jax.nn.softplus(-(lp['w0'] + jnp.tanh(wd) @ lp['w2']).astype(f32)) - 0.5
    decay = jnp.exp(-jnp.exp(w_log))
    a = jax.nn.sigmoid((lp['a0'] + ad @ lp['a2']).astype(f32))
    g = jax.nn.sigmoid(gd) @ lp['g2']

    def heads(t):
        return t.reshape(B, T, RWKV_HEADS, RWKV_HEAD)

    kk = heads((kr * lp['k_k']).astype(f32))
    kk = kk / jnp.maximum(jnp.linalg.norm(kk, axis=-1, keepdims=True), 1e-12)
    kr = kr.astype(f32) * (1.0 + (a - 1.0) * lp['k_a'])
    s_new, y = rwkv7_scan(rwkv_s0, heads(r), heads(decay), heads(kr), heads(vr), -kk, kk * heads(a))
    mu = jnp.mean(y, -1, keepdims=True)
    var = jnp.mean(jnp.square(y - mu), -1, keepdims=True)
    y = ((y - mu) * lax.rsqrt(var + GN_EPS)).reshape(B, T, RWKV_WIDTH) * lp['lnx_g'] + lp['lnx_b']
    bonus = jnp.sum(heads(r.astype(f32) * kr * lp['r_k']), -1, keepdims=True) * heads(vr).astype(f32)
    y_b = ((y + bonus.reshape(B, T, RWKV_WIDTH)) * g).astype(x.dtype) @ lp['p_b']

    gate_a = jax.nn.sigmoid(z[..., G_OFF:G_OFF + D_MODEL])
    gate_b = jax.nn.sigmoid(z[..., G_OFF + D_MODEL:])
    m = (gate_a * y_a + gate_b * y_b) @ lp['w_out']
    return m, (k, v, s_new.astype(x.dtype), zs[:, -1])


def decoder_layer(x, lp, lam_init, attend, rwkv_s0, shift0):
    x = layer_norm(ALPHA * x + 0.5 * swiglu(x, lp['w_ffn1_in'], lp['w_ffn1_out']), lp['ln1_g'], lp['ln1_b'])
    m, new_state = token_mixing(x, lp, lam_init, attend, rwkv_s0, shift0)
    x = layer_norm(ALPHA * x + m, lp['ln2_g'], lp['ln2_b'])
    x = layer_norm(ALPHA * x + 0.5 * swiglu(x, lp['w_ffn2_in'], lp['w_ffn2_out']), lp['ln3_g'], lp['ln3_b'])
    return x, new_state


def setup_inputs(seed: int = 0) -> dict:
    key = jax.random.key(seed)
    ks = iter(jax.random.split(key, 64))
    f32 = jnp.float32

    def nrm(shape, scale):
        return jax.random.normal(next(ks), shape, f32) * scale

    def gain(shape):
        return 1.0 + nrm(shape, 0.02)

    n_pages = PAST_LEN // PAGE_SIZE
    n_used = DEC_BATCH * n_pages
    n_pool = n_used + n_used // 4
    page_table = jax.random.permutation(next(ks), n_pool)[:n_used].reshape(DEC_BATCH, n_pages).astype(jnp.int32)

    col_scale = (jnp.ones((IN_COLS,), f32).at[V_OFF:SH_OFF].set(BETA)
                 .at[SH_OFF + 2 * RWKV_WIDTH:SH_OFF + 3 * RWKV_WIDTH].set(BETA))
    ratio = jnp.arange(RWKV_WIDTH, dtype=f32) / (RWKV_WIDTH - 1)
    w0_ramp = -7.0 + 5.0 * ratio ** 0.85 + 0.5
    L, D = DEPTH, D_MODEL
    return {
        "x_prompt": nrm((BATCH, SEQ, D), 1.0),
        "x_sample": nrm((DEC_BATCH, DEC_SEQ, D), 1.0),
        "cache_k": nrm((L, n_pool, PAGE_SIZE, ATT_HEADS, 2 * ATT_HEAD_DIM), 1.0),
        "cache_v": nrm((L, n_pool, PAGE_SIZE, ATT_HEADS, ATT_V_DIM), BETA),
        "state_rwkv": nrm((L, DEC_BATCH, RWKV_HEADS, RWKV_HEAD, RWKV_HEAD), 1.0),
        "state_shift": nrm((L, DEC_BATCH, SHIFT_W), 1.0),
        "page_table": page_table,
        "w_ffn1_in": nrm((L, D, 2 * D_FF), BETA * D ** -0.5),
        "w_ffn1_out": nrm((L, D_FF, D), BETA * D_FF ** -0.5),
        "ln1_g": gain((L, D)),
        "ln1_b": nrm((L, D), 0.02),
        "w_in": nrm((L, D, IN_COLS), D ** -0.5) * col_scale,
        "lambda_q1": nrm((L, ATT_HEAD_DIM), 0.1),
        "lambda_k1": nrm((L, ATT_HEAD_DIM), 0.1),
        "lambda_q2": nrm((L, ATT_HEAD_DIM), 0.1),
        "lambda_k2": nrm((L, ATT_HEAD_DIM), 0.1),
        "subln_g": gain((L, ATT_V_DIM)),
        "p_a": nrm((L, ATT_WIDTH, D), BETA * ATT_WIDTH ** -0.5),
        "mu_shift": jax.random.uniform(next(ks), (L, SHIFT_W), f32),
        "w0": w0_ramp + nrm((L, RWKV_WIDTH), 0.01),
        "w2": nrm((L, DECAY_LORA, RWKV_WIDTH), DECAY_LORA ** -0.5),
        "a0": nrm((L, RWKV_WIDTH), 0.1),
        "a2": nrm((L, ICLR_LORA, RWKV_WIDTH), ICLR_LORA ** -0.5),
        "g2": nrm((L, GATE_LORA, RWKV_WIDTH), GATE_LORA ** -0.5),
        "k_k": 0.85 + nrm((L, RWKV_WIDTH), 0.02),
        "k_a": gain((L, RWKV_WIDTH)),
        "r_k": nrm((L, RWKV_WIDTH), 0.1),
        "lnx_g": gain((L, RWKV_WIDTH)),
        "lnx_b": nrm((L, RWKV_WIDTH), 0.02),
        "p_b": nrm((L, RWKV_WIDTH, D), BETA * RWKV_WIDTH ** -0.5),
        "w_out": nrm((L, D, D), BETA * D ** -0.5),
        "ln2_g": gain((L, D)),
        "ln2_b": nrm((L, D), 0.02),
        "w_ffn2_in": nrm((L, D, 2 * D_FF), BETA * D ** -0.5),
        "w_ffn2_out": nrm((L, D_FF, D), BETA * D_FF ** -0.5),
        "ln3_g": gain((L, D)),
        "ln3_b": nrm((L, D), 0.02),
    }


def reference(x_prompt, x_sample, cache_k, cache_v, state_rwkv, state_shift, page_table,
              w_ffn1_in, w_ffn1_out, ln1_g, ln1_b, w_in, lambda_q1, lambda_k1, lambda_q2, lambda_k2,
              subln_g, p_a, mu_shift, w0, w2, a0, a2, g2, k_k, k_a, r_k, lnx_g, lnx_b, p_b, w_out,
              ln2_g, ln2_b, w_ffn2_in, w_ffn2_out, ln3_g, ln3_b):
    yp, ys = x_prompt, x_sample
    bp = x_prompt.shape[0]
    outs_p, outs_s = [], []
    for l in range(DEPTH):
        lp = dict(w_ffn1_in=w_ffn1_in[l], w_ffn1_out=w_ffn1_out[l], ln1_g=ln1_g[l], ln1_b=ln1_b[l],
                  w_in=w_in[l], lambda_q1=lambda_q1[l], lambda_k1=lambda_k1[l],
                  lambda_q2=lambda_q2[l], lambda_k2=lambda_k2[l], subln_g=subln_g[l], p_a=p_a[l],
                  mu_shift=mu_shift[l], w0=w0[l], w2=w2[l], a0=a0[l], a2=a2[l], g2=g2[l],
                  k_k=k_k[l], k_a=k_a[l], r_k=r_k[l], lnx_g=lnx_g[l], lnx_b=lnx_b[l], p_b=p_b[l],
                  w_out=w_out[l], ln2_g=ln2_g[l], ln2_b=ln2_b[l], w_ffn2_in=w_ffn2_in[l],
                  w_ffn2_out=w_ffn2_out[l], ln3_g=ln3_g[l], ln3_b=ln3_b[l])
        lam_init = 0.8 - 0.6 * math.exp(-0.3 * l)
        yp, st_p = decoder_layer(yp, lp, lam_init, prompt_attention,
                                 jnp.zeros((bp, RWKV_HEADS, RWKV_HEAD, RWKV_HEAD), yp.dtype),
                                 jnp.zeros((bp, SHIFT_W), yp.dtype))
        attend_s = functools.partial(sample_attention, cache_k=cache_k[l], cache_v=cache_v[l],
                                     page_table=page_table)
        ys, st_s = decoder_layer(ys, lp, lam_init, attend_s, state_rwkv[l], state_shift[l])
        outs_p.append(st_p)
        outs_s.append(st_s)
    k_p, v_p, s_p, sh_p = (jnp.stack(t) for t in zip(*outs_p))
    k_s, v_s, s_s, sh_s = (jnp.stack(t) for t in zip(*outs_s))
    return (yp, ys, k_p, v_p, s_p, sh_p, k_s, v_s, s_s, sh_s)
```

```python
import functools
import math

import jax
import jax.numpy as jnp
from jax import lax
from jax.experimental import pallas as pl
from jax.experimental.pallas import tpu as pltpu

F32 = jnp.float32
BF16 = jnp.bfloat16

LANES = 128
SUBLANES = 8
MXU_DIM = 256
VMEM_LIMIT_BYTES = 56 * 1024 * 1024

ATT_HEAD_DIM = 128
ATT_V_DIM = 2 * ATT_HEAD_DIM
ATT_SCALE = ATT_HEAD_DIM ** -0.5
NEG_INF = -1e30
RMS_EPS = 1e-5
RWKV_HEAD = 64
GN_EPS = 64e-5
LN_EPS = 1e-5
HIGHEST = lax.Precision.HIGHEST


def _round_up(x, m):
    return (x + m - 1) // m * m


def _params(*semantics):
    return pltpu.CompilerParams(dimension_semantics=semantics,
                                vmem_limit_bytes=VMEM_LIMIT_BYTES)


def _mm_kernel(*refs, n_extra, epilogue):
    x_ref, w_ref = refs[0], refs[1]
    extras = refs[2:2 + n_extra]
    outs = refs[2 + n_extra:]
    acc = jnp.dot(x_ref[...].astype(BF16), w_ref[...], preferred_element_type=F32)
    res = epilogue(acc, *[e[...] for e in extras])
    for o_ref, r in zip(outs, res):
        o_ref[...] = r.astype(o_ref.dtype)


def _matmul(x, w, *, tm, tn, epilogue, out_dtypes, extras=(), n_cols=None):
    m, k = x.shape
    n = w.shape[1] if n_cols is None else n_cols
    assert m % tm == 0 and n % tn == 0 and w.shape[0] == k
    in_specs = [pl.BlockSpec((tm, k), lambda i, j: (i, 0)),
                pl.BlockSpec((k, tn), lambda i, j: (0, j))]
    args = [x, w]
    for arr, off in extras:
        in_specs.append(pl.BlockSpec((tm, tn), lambda i, j, off=off: (i, j + off)))
        args.append(arr)
    return pl.pallas_call(
        functools.partial(_mm_kernel, n_extra=len(extras), epilogue=epilogue),
        grid=(m // tm, n // tn),
        in_specs=in_specs,
        out_specs=[pl.BlockSpec((tm, tn), lambda i, j: (i, j)) for _ in out_dtypes],
        out_shape=[jax.ShapeDtypeStruct((m, n), dt) for dt in out_dtypes],
        compiler_params=_params("parallel", "arbitrary"),
    )(*args)


def _ffn_in_kernel(x_ref, wg_ref, wu_ref, h_ref):
    x = x_ref[...]
    g = jnp.dot(x, wg_ref[...], preferred_element_type=F32)
    u = jnp.dot(x, wu_ref[...], preferred_element_type=F32)
    h_ref[...] = (g * jax.nn.sigmoid(g) * u).astype(h_ref.dtype)


def _ffn_in(x, w, *, tm, tn):
    m, d = x.shape
    f = w.shape[1] // 2
    assert m % tm == 0 and f % tn == 0
    nf = f // tn
    return pl.pallas_call(
        _ffn_in_kernel,
        grid=(m // tm, nf),
        in_specs=[pl.BlockSpec((tm, d), lambda i, j: (i, 0)),
                  pl.BlockSpec((d, tn), lambda i, j: (0, j)),
                  pl.BlockSpec((d, tn), lambda i, j: (0, j + nf))],
        out_specs=pl.BlockSpec((tm, tn), lambda i, j: (i, j)),
        out_shape=jax.ShapeDtypeStruct((m, f), BF16),
        compiler_params=_params("parallel", "arbitrary"),
    )(x, w, w)


def _ln_kernel(x_ref, g_ref, b_ref, o_ref, ob_ref):
    x = x_ref[...]
    mu = jnp.mean(x, axis=-1, keepdims=True)
    xc = x - mu
    var = jnp.mean(xc * xc, axis=-1, keepdims=True)
    y = xc * lax.rsqrt(var + LN_EPS) * g_ref[...] + b_ref[...]
    o_ref[...] = y
    ob_ref[...] = y.astype(BF16)


def _layer_norm(x, g, b, *, tr):
    m, d = x.shape
    assert m % tr == 0
    row = pl.BlockSpec((tr, d), lambda i: (i, 0))
    vec = pl.BlockSpec((1, d), lambda i: (0, 0))
    return pl.pallas_call(
        _ln_kernel,
        grid=(m // tr,),
        in_specs=[row, vec, vec],
        out_specs=[row, row],
        out_shape=[jax.ShapeDtypeStruct((m, d), F32), jax.ShapeDtypeStruct((m, d), BF16)],
        compiler_params=_params("parallel"),
    )(x, g.reshape(1, d), b.reshape(1, d))


def _swiglu_block(x, xb, w_in_b, w_out_b, g, b, alpha):
    h = _ffn_in(xb, w_in_b, tm=1024, tn=256)
    (pre,) = _matmul(h, w_out_b, tm=512, tn=256,
                     epilogue=lambda acc, xr: (alpha * xr + 0.5 * acc,),
                     out_dtypes=(F32,), extras=((x, 0),))
    return _layer_norm(pre, g, b, tr=256)


def _lambda_value(lq1, lk1, lq2, lk2, lam_init):
    return (jnp.exp(jnp.sum(lq1[...] * lk1[...], keepdims=True))
            - jnp.exp(jnp.sum(lq2[...] * lk2[...], keepdims=True)) + lam_init)


def _sub_norm(o, g, lam_init):
    ms = jnp.mean(o * o, axis=-1, keepdims=True)
    return o * lax.rsqrt(ms + RMS_EPS) * g * (1.0 - lam_init)


def _online_softmax_step(s, v, m_ref, l_ref, acc_ref):
    m_prev = m_ref[...]
    m_new = jnp.maximum(m_prev, jnp.max(s, axis=-1, keepdims=True))
    p = jnp.exp(s - m_new)
    corr = jnp.exp(m_prev - m_new)
    l_ref[...] = corr * l_ref[...] + jnp.sum(p, axis=-1, keepdims=True)
    acc_ref[...] = corr * acc_ref[...] + jnp.dot(p.astype(BF16), v, preferred_element_type=F32)
    m_ref[...] = m_new


def _prompt_attn_kernel(lq1, lk1, lq2, lk2, g_ref, q_ref, k_ref, v_ref, o_ref,
                        m1, l1, a1, m2, l2, a2, *, tq, nk, lam_init):
    qi = pl.program_id(2)
    ki = pl.program_id(3)

    @pl.when(ki == 0)
    def _():
        for m_ref, l_ref, a_ref in ((m1, l1, a1), (m2, l2, a2)):
            m_ref[...] = jnp.full(m_ref.shape, NEG_INF, F32)
            l_ref[...] = jnp.zeros(l_ref.shape, F32)
            a_ref[...] = jnp.zeros(a_ref.shape, F32)

    @pl.when(ki <= qi)
    def _():
        q = q_ref[...]
        k = k_ref[...]
        v = v_ref[...].astype(BF16)
        rows = qi * tq + lax.broadcasted_iota(jnp.int32, (tq, tq), 0)
        cols = ki * tq + lax.broadcasted_iota(jnp.int32, (tq, tq), 1)
        mask = cols <= rows
        for lo, m_ref, l_ref, a_ref in ((0, m1, l1, a1), (ATT_HEAD_DIM, m2, l2, a2)):
            s = lax.dot_general(q[:, lo:lo + ATT_HEAD_DIM].astype(BF16),
                                k[:, lo:lo + ATT_HEAD_DIM].astype(BF16),
                                (((1,), (1,)), ((), ())), preferred_element_type=F32) * ATT_SCALE
            s = jnp.where(mask, s, NEG_INF)
            _online_softmax_step(s, v, m_ref, l_ref, a_ref)

    @pl.when(ki == nk - 1)
    def _():
        lam = _lambda_value(lq1, lk1, lq2, lk2, lam_init)
        o = a1[...] / l1[...] - lam * (a2[...] / l2[...])
        o_ref[...] = _sub_norm(o, g_ref[...], lam_init).astype(o_ref.dtype)


def _prompt_attention(z, lam_vecs, subln_g, lam_init, *, batch, seq, heads, tq):
    nq = seq // tq
    kv_off = heads
    vec = pl.BlockSpec((1, ATT_HEAD_DIM), lambda b, h, qi, ki: (0, 0))
    blk = (tq, ATT_V_DIM)
    return pl.pallas_call(
        functools.partial(_prompt_attn_kernel, tq=tq, nk=nq, lam_init=lam_init),
        grid=(batch, heads, nq, nq),
        in_specs=[vec, vec, vec, vec,
                  pl.BlockSpec((1, ATT_V_DIM), lambda b, h, qi, ki: (0, 0)),
                  pl.BlockSpec(blk, lambda b, h, qi, ki: (b * nq + qi, h)),
                  pl.BlockSpec(blk, lambda b, h, qi, ki: (b * nq + jnp.minimum(ki, qi), kv_off + h)),
                  pl.BlockSpec(blk, lambda b, h, qi, ki: (b * nq + jnp.minimum(ki, qi), 2 * kv_off + h))],
        out_specs=pl.BlockSpec(blk, lambda b, h, qi, ki: (b * nq + qi, h)),
        out_shape=jax.ShapeDtypeStruct((batch * seq, heads * ATT_V_DIM), F32),
        scratch_shapes=[pltpu.VMEM((tq, 1), F32), pltpu.VMEM((tq, 1), F32), pltpu.VMEM((tq, ATT_V_DIM), F32),
                        pltpu.VMEM((tq, 1), F32), pltpu.VMEM((tq, 1), F32), pltpu.VMEM((tq, ATT_V_DIM), F32)],
        compiler_params=_params("parallel", "parallel", "parallel", "arbitrary"),
    )(*lam_vecs, subln_g.reshape(1, ATT_V_DIM), z, z, z)


def _sample_attn_kernel(*refs, n_pages, heads, t_new, page, lam_init):
    pt_ref = refs[0]
    lq1, lk1, lq2, lk2, g_ref, q_ref, kn_ref, vn_ref = refs[1:9]
    k_refs = refs[9:9 + n_pages]
    v_refs = refs[9 + n_pages:9 + 2 * n_pages]
    o_ref = refs[9 + 2 * n_pages]
    qbd, kb, vb, m_ref, l_ref, acc_ref = refs[10 + 2 * n_pages:]
    del pt_ref
    j = pl.program_id(1)
    rows = 2 * heads * t_new
    width = heads * ATT_V_DIM

    @pl.when(j == 0)
    def _():
        q = jnp.concatenate([q_ref[...]] * (2 * heads), axis=0)
        r_grp = lax.broadcasted_iota(jnp.int32, (rows, width), 0) // t_new
        c_grp = lax.broadcasted_iota(jnp.int32, (rows, width), 1) // ATT_HEAD_DIM
        qbd[...] = jnp.where(r_grp == c_grp, q, 0.0).astype(BF16)
        m_ref[...] = jnp.full(m_ref.shape, NEG_INF, F32)
        l_ref[...] = jnp.zeros(l_ref.shape, F32)
        acc_ref[...] = jnp.zeros(acc_ref.shape, F32)

    def update(s, vmat):
        m_prev = m_ref[...]
        m_new = jnp.maximum(m_prev, jnp.max(s, axis=-1, keepdims=True))
        p32 = jnp.exp(s - m_new)
        p = p32.astype(BF16)
        corr = jnp.exp(m_prev - m_new)
        l_ref[...] = corr * l_ref[...] + jnp.sum(p32, axis=-1, keepdims=True)
        g = 2 * t_new
        for h in range(heads):
            rs = slice(h * g, (h + 1) * g)
            pv = jnp.dot(p[rs, :], vmat[:, h * ATT_V_DIM:(h + 1) * ATT_V_DIM],
                         preferred_element_type=F32)
            acc_ref[rs, :] = corr[rs, :] * acc_ref[rs, :] + pv
        m_ref[...] = m_new

    for i in range(n_pages):
        kb[i * page:(i + 1) * page, :] = k_refs[i][...].astype(BF16)
        vb[i * page:(i + 1) * page, :] = v_refs[i][...].astype(BF16)
    s = lax.dot_general(qbd[...], kb[...], (((1,), (1,)), ((), ())),
                        preferred_element_type=F32) * ATT_SCALE
    update(s, vb[...])

    @pl.when(j == pl.num_programs(1) - 1)
    def _():
        pad = jnp.zeros((LANES - t_new, width), F32)
        kn = jnp.concatenate([kn_ref[...], pad], axis=0).astype(BF16)
        vn = jnp.concatenate([vn_ref[...], pad], axis=0).astype(BF16)
        sn = lax.dot_general(qbd[...], kn, (((1,), (1,)), ((), ())),
                             preferred_element_type=F32) * ATT_SCALE
        qpos = lax.broadcasted_iota(jnp.int32, (rows, LANES), 0) % t_new
        kpos = lax.broadcasted_iota(jnp.int32, (rows, LANES), 1)
        sn = jnp.where(kpos <= qpos, sn, NEG_INF)
        update(sn, vn)
        lam = _lambda_value(lq1, lk1, lq2, lk2, lam_init)
        o = acc_ref[...] / l_ref[...]
        g = 2 * t_new
        for h in range(heads):
            oh = o[h * g:h * g + t_new, :] - lam * o[h * g + t_new:(h + 1) * g, :]
            o_ref[:, h * ATT_V_DIM:(h + 1) * ATT_V_DIM] = _sub_norm(oh, g_ref[...], lam_init)


def _sample_attention(z, row0, cache_k, cache_v, page_table, lam_vecs, subln_g, lam_init,
                      *, heads, t_new, pages_per_step):
    n_seq, n_logical = page_table.shape
    n_pool, page = cache_k.shape[0], cache_k.shape[1]
    width = heads * ATT_V_DIM
    rows = 2 * heads * t_new
    assert rows == LANES and t_new == SUBLANES and n_logical % pages_per_step == 0
    assert row0 % t_new == 0
    ck = cache_k.reshape(n_pool, page, width)
    cv = cache_v.reshape(n_pool, page, width)
    nsteps = n_logical // pages_per_step
    rb0 = row0 // t_new
    vec = pl.BlockSpec((1, ATT_HEAD_DIM), lambda b, j, pt: (0, 0))

    def page_spec(i):
        return pl.BlockSpec((None, page, width),
                            lambda b, j, pt, i=i: (pt[b * n_logical + j * pages_per_step + i], 0, 0))

    new_spec = lambda c: pl.BlockSpec((t_new, width), lambda b, j, pt, c=c: (rb0 + b, c))
    grid_spec = pltpu.PrefetchScalarGridSpec(
        num_scalar_prefetch=1,
        grid=(n_seq, nsteps),
        in_specs=[vec, vec, vec, vec,
                  pl.BlockSpec((1, ATT_V_DIM), lambda b, j, pt: (0, 0)),
                  new_spec(0), new_spec(1), new_spec(2)]
                 + [page_spec(i) for i in range(pages_per_step)]
                 + [page_spec(i) for i in range(pages_per_step)],
        out_specs=pl.BlockSpec((t_new, width), lambda b, j, pt: (b, 0)),
        scratch_shapes=[pltpu.VMEM((rows, width), BF16),
                        pltpu.VMEM((pages_per_step * page, width), BF16),
                        pltpu.VMEM((pages_per_step * page, width), BF16),
                        pltpu.VMEM((rows, 1), F32), pltpu.VMEM((rows, 1), F32),
                        pltpu.VMEM((rows, ATT_V_DIM), F32)])
    return pl.pallas_call(
        functools.partial(_sample_attn_kernel, n_pages=pages_per_step, heads=heads,
                          t_new=t_new, page=page, lam_init=lam_init),
        grid_spec=grid_spec,
        out_shape=jax.ShapeDtypeStruct((n_seq * t_new, width), F32),
        compiler_params=_params("parallel", "arbitrary"),
    )(page_table.reshape(-1), *lam_vecs, subln_g.reshape(1, ATT_V_DIM), z, z, z,
      *([ck] * pages_per_step), *([cv] * pages_per_step))


def _head_sum(x, e_ref, et_ref):
    s = jnp.dot(x, e_ref[...], preferred_element_type=F32, precision=HIGHEST)
    return jnp.dot(s, et_ref[...], preferred_element_type=F32, precision=HIGHEST)


def _rwkv_prep_kernel(zs_ref, prev_ref, init_ref, mu_ref, w0_ref, w2_ref, a0_ref, a2_ref, g2_ref,
                      kk_ref, ka_ref, rk_ref, e_ref, et_ref,
                      r_out, w_out, k_out, v_out, a_out, b_out, g_out, bonus_out,
                      *, tb, seq_len, width, lora_w, lora_a):
    zs = zs_ref[...]
    row = lax.broadcasted_iota(jnp.int32, (tb, 1), 0)
    shifted = pltpu.roll(zs, 1, 0)
    if seq_len >= tb:
        at_start = (pl.program_id(0) * tb) % seq_len == 0
        carry = jnp.where(at_start, init_ref[0:1, :], prev_ref[SUBLANES - 1:SUBLANES, :])
        z_prev = jnp.where(row == 0, carry, shifted)
    else:
        z_prev = jnp.where(row % seq_len == 0, init_ref[...], shifted)
    zm = zs + (z_prev - zs) * mu_ref[...]
    r = zm[:, 0:width]
    kr = zm[:, width:2 * width]
    vr = zm[:, 2 * width:3 * width]
    o = 3 * width
    wd = zm[:, o:o + lora_w]
    ad = zm[:, o + lora_w:o + lora_w + lora_a]
    gd = zm[:, o + lora_w + lora_a:]
    wl = w0_ref[...] + jnp.dot(jnp.tanh(wd).astype(BF16), w2_ref[...], preferred_element_type=F32)
    softplus = jnp.maximum(-wl, 0.0) + jnp.log1p(jnp.exp(-jnp.abs(wl)))
    w_log = -softplus - 0.5
    decay = jnp.exp(-jnp.exp(w_log))
    a = jax.nn.sigmoid(a0_ref[...] + jnp.dot(ad.astype(BF16), a2_ref[...], preferred_element_type=F32))
    g = jnp.dot(jax.nn.sigmoid(gd).astype(BF16), g2_ref[...], preferred_element_type=F32)
    kk = kr * kk_ref[...]
    norm = jnp.sqrt(_head_sum(kk * kk, e_ref, et_ref))
    kk = kk / jnp.maximum(norm, 1e-12)
    kmod = kr * (1.0 + (a - 1.0) * ka_ref[...])
    bonus = _head_sum(r * kmod * rk_ref[...], e_ref, et_ref) * vr
    r_out[...] = r
    w_out[...] = decay
    k_out[...] = kmod
    v_out[...] = vr
    a_out[...] = -kk
    b_out[...] = kk * a
    g_out[...] = g
    bonus_out[...] = bonus


def _rwkv_prep(z, col0, zs_cols, row0, n_rows, init_rows, seq_len, lp, e, et, *, tb, width, lora_w, lora_a):
    assert n_rows % tb == 0 and row0 % tb == 0 and col0 % zs_cols == 0 and tb % SUBLANES == 0
    cb = col0 // zs_cols
    rb0 = row0 // tb
    nb = n_rows // tb
    per8 = tb // SUBLANES
    zs_spec = pl.BlockSpec((tb, zs_cols), lambda i: (rb0 + i, cb))
    prev_spec = pl.BlockSpec((SUBLANES, zs_cols),
                             lambda i: (jnp.maximum((rb0 + i) * per8 - 1, 0), cb))
    if seq_len >= tb:
        assert seq_len % tb == 0
        spb = seq_len // tb
        init_spec = pl.BlockSpec((SUBLANES, zs_cols), lambda i: (i // spb, 0))
    else:
        assert tb % seq_len == 0
        init_spec = pl.BlockSpec((tb, zs_cols), lambda i: (i, 0))
    full = lambda arr: pl.BlockSpec(arr.shape, lambda i: (0,) * arr.ndim)
    consts = [lp['mu'], lp['w0'], lp['w2'], lp['a0'], lp['a2'], lp['g2'],
              lp['k_k'], lp['k_a'], lp['r_k'], e, et]
    out_spec = pl.BlockSpec((tb, width), lambda i: (i, 0))
    return pl.pallas_call(
        functools.partial(_rwkv_prep_kernel, tb=tb, seq_len=seq_len, width=width,
                          lora_w=lora_w, lora_a=lora_a),
        grid=(nb,),
        in_specs=[zs_spec, prev_spec, init_spec] + [full(c) for c in consts],
        out_specs=[out_spec] * 8,
        out_shape=[jax.ShapeDtypeStruct((n_rows, width), F32)] * 8,
        compiler_params=_params("parallel"),
    )(z, z, init_rows, *consts)


def _rwkv_scan_kernel(r_ref, w_ref, k_ref, a_ref, b_ref, v_ref, s0_ref, y_ref, sout_ref, s_scr,
                      *, tb, n_key, n_val):
    t_blk = pl.program_id(1)
    nv = n_val // SUBLANES

    @pl.when(t_blk == 0)
    def _():
        s_scr[...] = s0_ref[0]

    def bcast(ref, t, c):
        return jnp.broadcast_to(ref[0, t, pl.ds(c, 1), :], (SUBLANES, LANES))

    def token(t, carry):
        sa = [jnp.zeros((SUBLANES, LANES), F32) for _ in range(nv)]
        for c in range(n_key):
            a_c = bcast(a_ref, t, c)
            for j in range(nv):
                sa[j] = sa[j] + s_scr[c, j * SUBLANES:(j + 1) * SUBLANES, :] * a_c
        vt = [v_ref[0, t, j * SUBLANES:(j + 1) * SUBLANES, :] for j in range(nv)]
        y = [jnp.zeros((SUBLANES, LANES), F32) for _ in range(nv)]
        for c in range(n_key):
            w_c = bcast(w_ref, t, c)
            b_c = bcast(b_ref, t, c)
            k_c = bcast(k_ref, t, c)
            r_c = bcast(r_ref, t, c)
            for j in range(nv):
                rows = slice(j * SUBLANES, (j + 1) * SUBLANES)
                s_new = s_scr[c, rows, :] * w_c + sa[j] * b_c + vt[j] * k_c
                s_scr[c, rows, :] = s_new
                y[j] = y[j] + s_new * r_c
        for j in range(nv):
            y_ref[0, t, j * SUBLANES:(j + 1) * SUBLANES, :] = y[j]
        return carry

    lax.fori_loop(0, tb, token, 0)

    @pl.when(t_blk == pl.num_programs(1) - 1)
    def _():
        sout_ref[0] = s_scr[...]


def _rwkv_scan(r, w, k, a, b, v, s0, *, tb):
    groups, t_len, n_key, _ = r.shape
    n_val = v.shape[2]
    assert t_len % tb == 0
    kspec = pl.BlockSpec((1, tb, n_key, LANES), lambda g, t: (g, t, 0, 0))
    vspec = pl.BlockSpec((1, tb, n_val, LANES), lambda g, t: (g, t, 0, 0))
    sspec = pl.BlockSpec((1, n_key, n_val, LANES), lambda g, t: (g, 0, 0, 0))
    return pl.pallas_call(
        functools.partial(_rwkv_scan_kernel, tb=tb, n_key=n_key, n_val=n_val),
        grid=(groups, t_len // tb),
        in_specs=[kspec] * 5 + [vspec, sspec],
        out_specs=[vspec, sspec],
        out_shape=[jax.ShapeDtypeStruct(v.shape, F32), jax.ShapeDtypeStruct(s0.shape, F32)],
        scratch_shapes=[pltpu.VMEM((n_key, n_val, LANES), F32)],
        compiler_params=_params("parallel", "arbitrary"),
    )(r, w, k, a, b, v, s0)


def _rwkv_post_kernel(y_ref, bonus_ref, g_ref, lg_ref, lb_ref, e_ref, et_ref, o_ref, *, head):
    y = y_ref[...]
    mu = _head_sum(y, e_ref, et_ref) * (1.0 / head)
    yc = y - mu
    var = _head_sum(yc * yc, e_ref, et_ref) * (1.0 / head)
    yn = yc * lax.rsqrt(var + GN_EPS) * lg_ref[...] + lb_ref[...]
    o_ref[...] = ((yn + bonus_ref[...]) * g_ref[...]).astype(o_ref.dtype)


def _rwkv_post(y, bonus, g, lnx_g, lnx_b, e, et, *, tb):
    n, width = y.shape
    assert n % tb == 0
    row = pl.BlockSpec((tb, width), lambda i: (i, 0))
    full = lambda arr: pl.BlockSpec(arr.shape, lambda i: (0,) * arr.ndim)
    return pl.pallas_call(
        functools.partial(_rwkv_post_kernel, head=RWKV_HEAD),
        grid=(n // tb,),
        in_specs=[row, row, row, full(lnx_g), full(lnx_b), full(e), full(et)],
        out_specs=row,
        out_shape=jax.ShapeDtypeStruct((n, width), BF16),
        compiler_params=_params("parallel"),
    )(y, bonus, g, lnx_g, lnx_b, e, et)


def _to_lanes_prompt(x, batch, seq, heads, split):
    x = x.reshape(batch, seq, heads, RWKV_HEAD).transpose(1, 3, 0, 2)
    x = jnp.broadcast_to(x[..., None], x.shape + (split,))
    return x.reshape(1, seq, RWKV_HEAD, batch * heads * split)


def _rwkv_scan_prompt(r, w, k, a, b, v, batch, seq, heads):
    split = LANES // (batch * heads)
    assert split >= 1 and batch * heads * split == LANES and RWKV_HEAD % (split * SUBLANES) == 0
    vl = RWKV_HEAD // split
    rl, wl, kl, al, bl = (_to_lanes_prompt(t, batch, seq, heads, split) for t in (r, w, k, a, b))
    vv = v.reshape(batch, seq, heads, split, vl).transpose(1, 4, 0, 2, 3).reshape(1, seq, vl, LANES)
    s0 = jnp.zeros((1, RWKV_HEAD, vl, LANES), F32)
    y, s = _rwkv_scan(rl, wl, kl, al, bl, vv, s0, tb=32)
    y = y.reshape(seq, vl, batch, heads, split).transpose(2, 0, 3, 4, 1).reshape(batch * seq, heads * RWKV_HEAD)
    s = s.reshape(RWKV_HEAD, vl, batch, heads, split).transpose(2, 3, 4, 1, 0)
    return y, s.reshape(batch, heads, RWKV_HEAD, RWKV_HEAD)


def _rwkv_scan_sample(r, w, k, a, b, v, s0, n_seq, t_new, heads):
    assert n_seq == LANES
    to = lambda t: t.reshape(n_seq, t_new, heads, RWKV_HEAD).transpose(2, 1, 3, 0)
    s_l = s0.transpose(1, 3, 2, 0)
    y, s = _rwkv_scan(to(r), to(w), to(k), to(a), to(b), to(v), s_l, tb=t_new)
    y = y.transpose(3, 1, 0, 2).reshape(n_seq * t_new, heads * RWKV_HEAD)
    return y, s.transpose(3, 0, 2, 1)


def kernel(x_prompt, x_sample, cache_k, cache_v, state_rwkv, state_shift, page_table, w_ffn1_in, w_ffn1_out, ln1_g, ln1_b, w_in, lambda_q1, lambda_k1, lambda_q2, lambda_k2, subln_g, p_a, mu_shift, w0, w2, a0, a2, g2, k_k, k_a, r_k, lnx_g, lnx_b, p_b, w_out, ln2_g, ln2_b, w_ffn2_in, w_ffn2_out, ln3_g, ln3_b):
    batch, seq, d_model = x_prompt.shape
    n_seq, t_new, _ = x_sample.shape
    depth = w_in.shape[0]
    heads = cache_k.shape[3]
    att_w = heads * ATT_V_DIM
    rw = p_b.shape[1]
    rheads = rw // RWKV_HEAD
    lora_w, lora_a, lora_g = w2.shape[1], a2.shape[1], g2.shape[1]
    shift_w = state_shift.shape[2]
    assert shift_w == 3 * rw + lora_w + lora_a + lora_g and rw == att_w
    alpha = (2 * depth) ** 0.25
    n_p, n_s = batch * seq, n_seq * t_new
    n_tok = n_p + n_s

    col_blk = 1024
    sh_pad = _round_up(shift_w, col_blk)
    g_off = 3 * att_w
    sh_off = g_off + 2 * d_model
    assert sh_off % sh_pad == 0
    lora_g_pad = sh_pad - (3 * rw + lora_w + lora_a)
    src_sh, src_g = 3 * att_w, 3 * att_w + shift_w

    ch = jnp.arange(rw) // RWKV_HEAD
    e = (ch[:, None] == jnp.arange(LANES)[None, :]).astype(F32)
    et = e.T

    x = jnp.concatenate([x_prompt.reshape(n_p, d_model), x_sample.reshape(n_s, d_model)], axis=0)
    xb = x.astype(BF16)
    outs_p, outs_s = [], []
    for l in range(depth):
        lam_init = 0.8 - 0.6 * math.exp(-0.3 * l)
        w_in_l = w_in[l]
        w_in_p = jnp.concatenate(
            [w_in_l[:, :src_sh], w_in_l[:, src_g:], w_in_l[:, src_sh:src_g],
             jnp.zeros((d_model, sh_pad - shift_w), F32)], axis=1).astype(BF16)
        lam_vecs = [t[l].reshape(1, ATT_HEAD_DIM) for t in (lambda_q1, lambda_k1, lambda_q2, lambda_k2)]
        pad_cols = lambda t: jnp.pad(t, ((0, 0), (0, sh_pad - shift_w)))
        lp = dict(mu=pad_cols(mu_shift[l].reshape(1, shift_w)), w0=w0[l].reshape(1, rw), w2=w2[l].astype(BF16),
                  a0=a0[l].reshape(1, rw), a2=a2[l].astype(BF16),
                  g2=jnp.pad(g2[l], ((0, lora_g_pad - lora_g), (0, 0))).astype(BF16),
                  k_k=k_k[l].reshape(1, rw), k_a=k_a[l].reshape(1, rw), r_k=r_k[l].reshape(1, rw))

        x, xb = _swiglu_block(x, xb, w_ffn1_in[l].astype(BF16), w_ffn1_out[l].astype(BF16),
                              ln1_g[l], ln1_b[l], alpha)

        (z,) = _matmul(xb, w_in_p, tm=1024, tn=512, epilogue=lambda acc: (acc,), out_dtypes=(F32,))

        o_p = _prompt_attention(z, lam_vecs, subln_g[l], lam_init, batch=batch, seq=seq, heads=heads, tq=512)
        o_s = _sample_attention(z, n_p, cache_k[l], cache_v[l], page_table, lam_vecs, subln_g[l], lam_init,
                                heads=heads, t_new=t_new, pages_per_step=4)
        o = jnp.concatenate([o_p, o_s], axis=0)
        (y_a,) = _matmul(o, p_a[l].astype(BF16), tm=1024, tn=512, epilogue=lambda acc: (acc,), out_dtypes=(F32,))

        prep = functools.partial(_rwkv_prep, z, sh_off, sh_pad, lp=lp, e=e, et=et,
                                 width=rw, lora_w=lora_w, lora_a=lora_a)
        init_p = jnp.zeros((batch * SUBLANES, sh_pad), F32)
        init_s = jnp.zeros((n_s, sh_pad), F32).at[::t_new, :shift_w].set(state_shift[l])
        r_p, w_p, k_p, v_p, a_p, b_p, g_p, bonus_p = prep(0, n_p, init_p, seq, tb=128)
        r_s, w_s, k_s, v_s, a_s, b_s, g_s, bonus_s = prep(n_p, n_s, init_s, t_new, tb=128)
        y_p, st_p = _rwkv_scan_prompt(r_p, w_p, k_p, a_p, b_p, v_p, batch, seq, rheads)
        y_s, st_s = _rwkv_scan_sample(r_s, w_s, k_s, a_s, b_s, v_s, state_rwkv[l], n_seq, t_new, rheads)
        cat = lambda p, s: jnp.concatenate([p, s], axis=0)
        yb = _rwkv_post(cat(y_p, y_s), cat(bonus_p, bonus_s), cat(g_p, g_s),
                        lnx_g[l].reshape(1, rw), lnx_b[l].reshape(1, rw), e, et, tb=256)

        gblk = g_off // 512
        (merged,) = _matmul(
            yb, p_b[l].astype(BF16), tm=1024, tn=512,
            epilogue=lambda acc, ya, ga, gb: (jax.nn.sigmoid(ga) * ya + jax.nn.sigmoid(gb) * acc,),
            out_dtypes=(BF16,), extras=((y_a, 0), (z, gblk), (z, gblk + d_model // 512)))
        (pre,) = _matmul(merged, w_out[l].astype(BF16), tm=1024, tn=512,
                         epilogue=lambda acc, xr: (alpha * xr + acc,), out_dtypes=(F32,), extras=((x, 0),))
        x, xb = _layer_norm(pre, ln2_g[l], ln2_b[l], tr=256)

        x, xb = _swiglu_block(x, xb, w_ffn2_in[l].astype(BF16), w_ffn2_out[l].astype(BF16),
                              ln3_g[l], ln3_b[l], alpha)

        kv = lambda lo, n0, n1, shape: z[n0:n1, lo:lo + att_w].reshape(shape)
        outs_p.append((kv(att_w, 0, n_p, (batch, seq, heads, ATT_V_DIM)),
                       kv(2 * att_w, 0, n_p, (batch, seq, heads, ATT_V_DIM)),
                       st_p, z[seq - 1:n_p:seq, sh_off:sh_off + shift_w]))
        outs_s.append((kv(att_w, n_p, n_tok, (n_seq, t_new, heads, ATT_V_DIM)),
                       kv(2 * att_w, n_p, n_tok, (n_seq, t_new, heads, ATT_V_DIM)),
                       st_s, z[n_p + t_new - 1:n_tok:t_new, sh_off:sh_off + shift_w]))

    k_p, v_p, s_p, sh_p = (jnp.stack(t) for t in zip(*outs_p))
    k_s, v_s, s_s, sh_s = (jnp.stack(t) for t in zip(*outs_s))
    return (x[:n_p].reshape(batch, seq, d_model), x[n_p:].reshape(n_seq, t_new, d_model),
            k_p, v_p, s_p, sh_p, k_s, v_s, s_s, sh_s)
```

```python
import functools
import math

import jax
import jax.numpy as jnp
from jax import lax
from jax.experimental import pallas as pl
from jax.experimental.pallas import tpu as pltpu

F32 = jnp.float32
BF16 = jnp.bfloat16

LANES = 128
SUBLANES = 8
MXU_DIM = 256
VMEM_LIMIT_BYTES = 56 * 1024 * 1024

ATT_HEAD_DIM = 128
ATT_V_DIM = 2 * ATT_HEAD_DIM
ATT_SCALE = ATT_HEAD_DIM ** -0.5
NEG_INF = -1e30
RMS_EPS = 1e-5
RWKV_HEAD = 64
GN_EPS = 64e-5
LN_EPS = 1e-5
HIGHEST = lax.Precision.HIGHEST


def _round_up(x, m):
    return (x + m - 1) // m * m


def _params(*semantics):
    return pltpu.CompilerParams(dimension_semantics=semantics,
                                vmem_limit_bytes=VMEM_LIMIT_BYTES)


def _mm_kernel(*refs, n_extra, epilogue):
    x_ref, w_ref = refs[0], refs[1]
    extras = refs[2:2 + n_extra]
    outs = refs[2 + n_extra:]
    acc = jnp.dot(x_ref[...].astype(BF16), w_ref[...], preferred_element_type=F32)
    res = epilogue(acc, *[e[...] for e in extras])
    for o_ref, r in zip(outs, res):
        o_ref[...] = r.astype(o_ref.dtype)


def _matmul(x, w, *, tm, tn, epilogue, out_dtypes, extras=(), n_cols=None):
    m, k = x.shape
    n = w.shape[1] if n_cols is None else n_cols
    assert m % tm == 0 and n % tn == 0 and w.shape[0] == k
    in_specs = [pl.BlockSpec((tm, k), lambda i, j: (i, 0)),
                pl.BlockSpec((k, tn), lambda i, j: (0, j))]
    args = [x, w]
    for arr, off in extras:
        in_specs.append(pl.BlockSpec((tm, tn), lambda i, j, off=off: (i, j + off)))
        args.append(arr)
    return pl.pallas_call(
        functools.partial(_mm_kernel, n_extra=len(extras), epilogue=epilogue),
        grid=(m // tm, n // tn),
        in_specs=in_specs,
        out_specs=[pl.BlockSpec((tm, tn), lambda i, j: (i, j)) for _ in out_dtypes],
        out_shape=[jax.ShapeDtypeStruct((m, n), dt) for dt in out_dtypes],
        compiler_params=_params("parallel", "arbitrary"),
    )(*args)


def _ffn_in_kernel(x_ref, wg_ref, wu_ref, h_ref):
    x = x_ref[...]
    g = jnp.dot(x, wg_ref[...].astype(BF16), preferred_element_type=F32)
    u = jnp.dot(x, wu_ref[...].astype(BF16), preferred_element_type=F32)
    h_ref[...] = (g * jax.nn.sigmoid(g) * u).astype(h_ref.dtype)


def _ffn_in(x, w, *, tm, tn):
    m, d = x.shape
    f = w.shape[1] // 2
    assert m % tm == 0 and f % tn == 0
    nf = f // tn
    return pl.pallas_call(
        _ffn_in_kernel,
        grid=(m // tm, nf),
        in_specs=[pl.BlockSpec((tm, d), lambda i, j: (i, 0)),
                  pl.BlockSpec((d, tn), lambda i, j: (0, j)),
                  pl.BlockSpec((d, tn), lambda i, j: (0, j + nf))],
        out_specs=pl.BlockSpec((tm, tn), lambda i, j: (i, j)),
        out_shape=jax.ShapeDtypeStruct((m, f), BF16),
        compiler_params=_params("parallel", "arbitrary"),
    )(x, w, w)


def _ln_kernel(x_ref, g_ref, b_ref, o_ref, ob_ref):
    x = x_ref[...]
    mu = jnp.mean(x, axis=-1, keepdims=True)
    xc = x - mu
    var = jnp.mean(xc * xc, axis=-1, keepdims=True)
    y = xc * lax.rsqrt(var + LN_EPS) * g_ref[...] + b_ref[...]
    o_ref[...] = y
    ob_ref[...] = y.astype(BF16)


def _layer_norm(x, g, b, *, tr):
    m, d = x.shape
    assert m % tr == 0
    row = pl.BlockSpec((tr, d), lambda i: (i, 0))
    vec = pl.BlockSpec((1, d), lambda i: (0, 0))
    return pl.pallas_call(
        _ln_kernel,
        grid=(m // tr,),
        in_specs=[row, vec, vec],
        out_specs=[row, row],
        out_shape=[jax.ShapeDtypeStruct((m, d), F32), jax.ShapeDtypeStruct((m, d), BF16)],
        compiler_params=_params("parallel"),
    )(x, g.reshape(1, d), b.reshape(1, d))


def _swiglu_block(x, xb, w_in, w_out_b, g, b, alpha):
    h = _ffn_in(xb, w_in, tm=1024, tn=256)
    (pre,) = _matmul(h, w_out_b, tm=512, tn=256,
                     epilogue=lambda acc, xr: (alpha * xr + 0.5 * acc,),
                     out_dtypes=(F32,), extras=((x, 0),))
    return _layer_norm(pre, g, b, tr=256)


EXP2_SCALE = ATT_SCALE * math.log2(math.e)


def _lambda_value(lq1, lk1, lq2, lk2, lam_init):
    return (jnp.exp(jnp.sum(lq1[...] * lk1[...], keepdims=True))
            - jnp.exp(jnp.sum(lq2[...] * lk2[...], keepdims=True)) + lam_init)


def _sub_norm(o, g, lam_init):
    ms = jnp.mean(o * o, axis=-1, keepdims=True)
    return o * lax.rsqrt(ms + RMS_EPS) * g * (1.0 - lam_init)


def _online_softmax_step(s, v, m_ref, l_ref, acc_ref):
    m_prev = m_ref[...]
    m_new = jnp.maximum(m_prev, jnp.max(s, axis=-1, keepdims=True))
    p = jnp.exp2((s - m_new) * EXP2_SCALE)
    corr = jnp.exp2((m_prev - m_new) * EXP2_SCALE)
    l_ref[...] = corr * l_ref[...] + jnp.sum(p, axis=-1, keepdims=True)
    acc_ref[...] = corr * acc_ref[...] + jnp.dot(p.astype(BF16), v, preferred_element_type=F32)
    m_ref[...] = m_new


def _prompt_attn_kernel(lq1, lk1, lq2, lk2, g_ref, q_ref, k_ref, v_ref, o_ref,
                        kb, vb, m1, l1, a1, m2, l2, a2, *, tq, lam_init):
    qi = pl.program_id(2)
    branches = ((0, m1, l1, a1), (ATT_HEAD_DIM, m2, l2, a2))

    @pl.when(qi == 0)
    def _():
        kb[...] = k_ref[...].astype(BF16)
        vb[...] = v_ref[...].astype(BF16)

    for _, m_ref, l_ref, a_ref in branches:
        m_ref[...] = jnp.full(m_ref.shape, NEG_INF, F32)
        l_ref[...] = jnp.zeros(l_ref.shape, F32)
        a_ref[...] = jnp.zeros(a_ref.shape, F32)
    q = q_ref[...].astype(BF16)

    def kv_block(ki, causal):
        start = pl.multiple_of(ki * tq, tq)
        k = kb[pl.ds(start, tq), :]
        v = vb[pl.ds(start, tq), :]
        for lo, m_ref, l_ref, a_ref in branches:
            s = lax.dot_general(q[:, lo:lo + ATT_HEAD_DIM], k[:, lo:lo + ATT_HEAD_DIM],
                                (((1,), (1,)), ((), ())), preferred_element_type=F32)
            if causal:
                rows = lax.broadcasted_iota(jnp.int32, (tq, tq), 0)
                cols = lax.broadcasted_iota(jnp.int32, (tq, tq), 1)
                s = jnp.where(cols <= rows, s, NEG_INF)
            _online_softmax_step(s, v, m_ref, l_ref, a_ref)

    def body(ki, carry):
        kv_block(ki, False)
        return carry

    lax.fori_loop(0, qi, body, 0)
    kv_block(qi, True)

    lam = _lambda_value(lq1, lk1, lq2, lk2, lam_init)
    o = a1[...] / l1[...] - lam * (a2[...] / l2[...])
    o_ref[...] = _sub_norm(o, g_ref[...], lam_init).astype(o_ref.dtype)


def _prompt_attention(z, lam_vecs, subln_g, lam_init, *, batch, seq, heads, tq):
    nq = seq // tq
    kv_off = heads
    vec = pl.BlockSpec((1, ATT_HEAD_DIM), lambda b, h, qi: (0, 0))
    blk = (tq, ATT_V_DIM)
    full = (seq, ATT_V_DIM)
    return pl.pallas_call(
        functools.partial(_prompt_attn_kernel, tq=tq, lam_init=lam_init),
        grid=(batch, heads, nq),
        in_specs=[vec, vec, vec, vec,
                  pl.BlockSpec((1, ATT_V_DIM), lambda b, h, qi: (0, 0)),
                  pl.BlockSpec(blk, lambda b, h, qi: (b * nq + qi, h)),
                  pl.BlockSpec(full, lambda b, h, qi: (b, kv_off + h)),
                  pl.BlockSpec(full, lambda b, h, qi: (b, 2 * kv_off + h))],
        out_specs=pl.BlockSpec(blk, lambda b, h, qi: (b * nq + qi, h)),
        out_shape=jax.ShapeDtypeStruct((batch * seq, heads * ATT_V_DIM), F32),
        scratch_shapes=[pltpu.VMEM(full, BF16), pltpu.VMEM(full, BF16),
                        pltpu.VMEM((tq, 1), F32), pltpu.VMEM((tq, 1), F32), pltpu.VMEM((tq, ATT_V_DIM), F32),
                        pltpu.VMEM((tq, 1), F32), pltpu.VMEM((tq, 1), F32), pltpu.VMEM((tq, ATT_V_DIM), F32)],
        compiler_params=_params("parallel", "parallel", "arbitrary"),
    )(*lam_vecs, subln_g.reshape(1, ATT_V_DIM), z, z, z)


def _sample_attn_kernel(*refs, n_pages, heads, t_new, page, lam_init):
    pt_ref = refs[0]
    lq1, lk1, lq2, lk2, g_ref, q_ref, kn_ref, vn_ref = refs[1:9]
    k_refs = refs[9:9 + n_pages]
    v_refs = refs[9 + n_pages:9 + 2 * n_pages]
    o_ref = refs[9 + 2 * n_pages]
    qt, kb, vb, m_ref, l_ref, acc_ref = refs[10 + 2 * n_pages:]
    del pt_ref
    j = pl.program_id(1)
    rows = 2 * heads * t_new
    prow = page * heads

    @pl.when(j == 0)
    def _():
        q = q_ref[...]
        lane = lax.broadcasted_iota(jnp.int32, (t_new, ATT_V_DIM), 1)
        pieces = []
        for br in range(2):
            keep = (lane >= ATT_HEAD_DIM) if br else (lane < ATT_HEAD_DIM)
            for h in range(heads):
                pieces.append(jnp.where(keep, q[:, h * ATT_V_DIM:(h + 1) * ATT_V_DIM], 0.0))
        qt[...] = jnp.concatenate(pieces, axis=0).astype(BF16)
        m_ref[...] = jnp.full(m_ref.shape, NEG_INF, F32)
        l_ref[...] = jnp.zeros(l_ref.shape, F32)
        acc_ref[...] = jnp.zeros(acc_ref.shape, F32)

    def scores(keys, n_keys, causal):
        s = lax.dot_general(keys, qt[...], (((1,), (1,)), ((), ())), preferred_element_type=F32)
        s = s.T
        r_idx = lax.broadcasted_iota(jnp.int32, (rows, n_keys * heads), 0)
        c_idx = lax.broadcasted_iota(jnp.int32, (rows, n_keys * heads), 1)
        keep = (c_idx % heads) == ((r_idx // t_new) % heads)
        if causal:
            keep = keep & ((c_idx // heads) <= (r_idx % t_new))
        return jnp.where(keep, s, NEG_INF)

    for i in range(n_pages):
        kb[i * prow:(i + 1) * prow, :] = k_refs[i][...].reshape(prow, ATT_V_DIM).astype(BF16)
        vb[i * prow:(i + 1) * prow, :] = v_refs[i][...].reshape(prow, ATT_V_DIM).astype(BF16)
    _online_softmax_step(scores(kb[...], n_pages * page, False), vb[...], m_ref, l_ref, acc_ref)

    @pl.when(j == pl.num_programs(1) - 1)
    def _():
        def rows_of(ref):
            x = ref[...].reshape(t_new, heads, ATT_V_DIM).reshape(t_new * heads, ATT_V_DIM)
            pad = jnp.zeros((LANES - t_new * heads, ATT_V_DIM), F32)
            return jnp.concatenate([x, pad], axis=0).astype(BF16)
        _online_softmax_step(scores(rows_of(kn_ref), LANES // heads, True), rows_of(vn_ref),
                             m_ref, l_ref, acc_ref)
        lam = _lambda_value(lq1, lk1, lq2, lk2, lam_init)
        o = acc_ref[...] / l_ref[...]
        half = heads * t_new
        for h in range(heads):
            oh = o[h * t_new:(h + 1) * t_new, :] - lam * o[half + h * t_new:half + (h + 1) * t_new, :]
            o_ref[:, h * ATT_V_DIM:(h + 1) * ATT_V_DIM] = _sub_norm(oh, g_ref[...], lam_init)


def _sample_attention(z, row0, cache_k, cache_v, page0, page_table, lam_vecs, subln_g, lam_init,
                      *, heads, t_new, pages_per_step):
    n_seq, n_logical = page_table.shape
    page = cache_k.shape[1]
    width = heads * ATT_V_DIM
    rows = 2 * heads * t_new
    assert rows == LANES and t_new == SUBLANES and n_logical % pages_per_step == 0
    assert row0 % t_new == 0
    nsteps = n_logical // pages_per_step
    rb0 = row0 // t_new
    vec = pl.BlockSpec((1, ATT_HEAD_DIM), lambda b, j, pt: (0, 0))

    def page_spec(i):
        return pl.BlockSpec(
            (None, page, heads, ATT_V_DIM),
            lambda b, j, pt, i=i: (page0 + pt[b * n_logical + j * pages_per_step + i], 0, 0, 0))

    new_spec = lambda c: pl.BlockSpec((t_new, width), lambda b, j, pt, c=c: (rb0 + b, c))
    key_rows = pages_per_step * page * heads
    grid_spec = pltpu.PrefetchScalarGridSpec(
        num_scalar_prefetch=1,
        grid=(n_seq, nsteps),
        in_specs=[vec, vec, vec, vec,
                  pl.BlockSpec((1, ATT_V_DIM), lambda b, j, pt: (0, 0)),
                  new_spec(0), new_spec(1), new_spec(2)]
                 + [page_spec(i) for i in range(pages_per_step)]
                 + [page_spec(i) for i in range(pages_per_step)],
        out_specs=pl.BlockSpec((t_new, width), lambda b, j, pt: (b, 0)),
        scratch_shapes=[pltpu.VMEM((rows, ATT_V_DIM), BF16),
                        pltpu.VMEM((key_rows, ATT_V_DIM), BF16),
                        pltpu.VMEM((key_rows, ATT_V_DIM), BF16),
                        pltpu.VMEM((rows, 1), F32), pltpu.VMEM((rows, 1), F32),
                        pltpu.VMEM((rows, ATT_V_DIM), F32)])
    return pl.pallas_call(
        functools.partial(_sample_attn_kernel, n_pages=pages_per_step, heads=heads,
                          t_new=t_new, page=page, lam_init=lam_init),
        grid_spec=grid_spec,
        out_shape=jax.ShapeDtypeStruct((n_seq * t_new, width), F32),
        compiler_params=_params("parallel", "arbitrary"),
    )(page_table.reshape(-1), *lam_vecs, subln_g.reshape(1, ATT_V_DIM), z, z, z,
      *([cache_k] * pages_per_step), *([cache_v] * pages_per_step))


def _head_sum(x, e_ref, et_ref):
    s = jnp.dot(x, e_ref[...], preferred_element_type=F32, precision=HIGHEST)
    return jnp.dot(s, et_ref[...], preferred_element_type=F32, precision=HIGHEST)


def _rwkv_prep_kernel(zs_ref, prev_ref, init_ref, mu_ref, w0_ref, w2_ref, a0_ref, a2_ref, g2_ref,
                      kk_ref, ka_ref, rk_ref, e_ref, et_ref,
                      r_out, w_out, k_out, v_out, a_out, b_out, g_out, bonus_out,
                      *, tb, seq_len, width, lora_w, lora_a):
    zs = zs_ref[...]
    row = lax.broadcasted_iota(jnp.int32, (tb, 1), 0)
    shifted = pltpu.roll(zs, 1, 0)
    if seq_len >= tb:
        at_start = (pl.program_id(0) * tb) % seq_len == 0
        carry = jnp.where(at_start, init_ref[0:1, :], prev_ref[SUBLANES - 1:SUBLANES, :])
        z_prev = jnp.where(row == 0, carry, shifted)
    else:
        z_prev = jnp.where(row % seq_len == 0, init_ref[...], shifted)
    zm = zs + (z_prev - zs) * mu_ref[...]
    r = zm[:, 0:width]
    kr = zm[:, width:2 * width]
    vr = zm[:, 2 * width:3 * width]
    o = 3 * width
    wd = zm[:, o:o + lora_w]
    ad = zm[:, o + lora_w:o + lora_w + lora_a]
    gd = zm[:, o + lora_w + lora_a:]
    wl = w0_ref[...] + jnp.dot(jnp.tanh(wd).astype(BF16), w2_ref[...], preferred_element_type=F32)
    softplus = jnp.maximum(-wl, 0.0) + jnp.log1p(jnp.exp(-jnp.abs(wl)))
    w_log = -softplus - 0.5
    decay = jnp.exp(-jnp.exp(w_log))
    a = jax.nn.sigmoid(a0_ref[...] + jnp.dot(ad.astype(BF16), a2_ref[...], preferred_element_type=F32))
    g = jnp.dot(jax.nn.sigmoid(gd).astype(BF16), g2_ref[...], preferred_element_type=F32)
    kk = kr * kk_ref[...]
    norm = jnp.sqrt(_head_sum(kk * kk, e_ref, et_ref))
    kk = kk / jnp.maximum(norm, 1e-12)
    kmod = kr * (1.0 + (a - 1.0) * ka_ref[...])
    bonus = _head_sum(r * kmod * rk_ref[...], e_ref, et_ref) * vr
    r_out[...] = r
    w_out[...] = decay
    k_out[...] = kmod
    v_out[...] = vr
    a_out[...] = -kk
    b_out[...] = kk * a
    g_out[...] = g
    bonus_out[...] = bonus


def _rwkv_prep(z, col0, zs_cols, row0, n_rows, init_rows, seq_len, lp, e, et, *, tb, width, lora_w, lora_a):
    assert n_rows % tb == 0 and row0 % tb == 0 and col0 % zs_cols == 0 and tb % SUBLANES == 0
    cb = col0 // zs_cols
    rb0 = row0 // tb
    nb = n_rows // tb
    per8 = tb // SUBLANES
    zs_spec = pl.BlockSpec((tb, zs_cols), lambda i: (rb0 + i, cb))
    prev_spec = pl.BlockSpec((SUBLANES, zs_cols),
                             lambda i: (jnp.maximum((rb0 + i) * per8 - 1, 0), cb))
    if seq_len >= tb:
        assert seq_len % tb == 0
        spb = seq_len // tb
        init_spec = pl.BlockSpec((SUBLANES, zs_cols), lambda i: (i // spb, 0))
    else:
        assert tb % seq_len == 0
        init_spec = pl.BlockSpec((tb, zs_cols), lambda i: (i, 0))
    full = lambda arr: pl.BlockSpec(arr.shape, lambda i: (0,) * arr.ndim)
    consts = [lp['mu'], lp['w0'], lp['w2'], lp['a0'], lp['a2'], lp['g2'],
              lp['k_k'], lp['k_a'], lp['r_k'], e, et]
    out_spec = pl.BlockSpec((tb, width), lambda i: (i, 0))
    return pl.pallas_call(
        functools.partial(_rwkv_prep_kernel, tb=tb, seq_len=seq_len, width=width,
                          lora_w=lora_w, lora_a=lora_a),
        grid=(nb,),
        in_specs=[zs_spec, prev_spec, init_spec] + [full(c) for c in consts],
        out_specs=[out_spec] * 8,
        out_shape=[jax.ShapeDtypeStruct((n_rows, width), F32)] * 8,
        compiler_params=_params("parallel"),
    )(z, z, init_rows, *consts)


def _rwkv_scan_kernel(r_ref, w_ref, k_ref, a_ref, b_ref, v_ref, s0_ref, y_ref, sout_ref, s_scr,
                      *, tb, n_key, n_val):
    t_blk = pl.program_id(1)
    nv = n_val // SUBLANES

    @pl.when(t_blk == 0)
    def _():
        s_scr[...] = s0_ref[0]

    def bcast(ref, t, c):
        return jnp.broadcast_to(ref[0, t, pl.ds(c, 1), :], (SUBLANES, LANES))

    def token(t, carry):
        sa = [jnp.zeros((SUBLANES, LANES), F32) for _ in range(nv)]
        for c in range(n_key):
            a_c = bcast(a_ref, t, c)
            for j in range(nv):
                sa[j] = sa[j] + s_scr[c, j * SUBLANES:(j + 1) * SUBLANES, :] * a_c
        vt = [v_ref[0, t, j * SUBLANES:(j + 1) * SUBLANES, :] for j in range(nv)]
        y = [jnp.zeros((SUBLANES, LANES), F32) for _ in range(nv)]
        for c in range(n_key):
            w_c = bcast(w_ref, t, c)
            b_c = bcast(b_ref, t, c)
            k_c = bcast(k_ref, t, c)
            r_c = bcast(r_ref, t, c)
            for j in range(nv):
                rows = slice(j * SUBLANES, (j + 1) * SUBLANES)
                s_new = s_scr[c, rows, :] * w_c + sa[j] * b_c + vt[j] * k_c
                s_scr[c, rows, :] = s_new
                y[j] = y[j] + s_new * r_c
        for j in range(nv):
            y_ref[0, t, j * SUBLANES:(j + 1) * SUBLANES, :] = y[j]
        return carry

    lax.fori_loop(0, tb, token, 0)

    @pl.when(t_blk == pl.num_programs(1) - 1)
    def _():
        sout_ref[0] = s_scr[...]


def _rwkv_scan(r, w, k, a, b, v, s0, *, tb):
    groups, t_len, n_key, _ = r.shape
    n_val = v.shape[2]
    assert t_len % tb == 0
    kspec = pl.BlockSpec((1, tb, n_key, LANES), lambda g, t: (g, t, 0, 0))
    vspec = pl.BlockSpec((1, tb, n_val, LANES), lambda g, t: (g, t, 0, 0))
    sspec = pl.BlockSpec((1, n_key, n_val, LANES), lambda g, t: (g, 0, 0, 0))
    return pl.pallas_call(
        functools.partial(_rwkv_scan_kernel, tb=tb, n_key=n_key, n_val=n_val),
        grid=(groups, t_len // tb),
        in_specs=[kspec] * 5 + [vspec, sspec],
        out_specs=[vspec, sspec],
        out_shape=[jax.ShapeDtypeStruct(v.shape, F32), jax.ShapeDtypeStruct(s0.shape, F32)],
        scratch_shapes=[pltpu.VMEM((n_key, n_val, LANES), F32)],
        compiler_params=_params("parallel", "arbitrary"),
    )(r, w, k, a, b, v, s0)


def _rwkv_post_kernel(y_ref, bonus_ref, g_ref, lg_ref, lb_ref, e_ref, et_ref, o_ref, *, head):
    y = y_ref[...]
    mu = _head_sum(y, e_ref, et_ref) * (1.0 / head)
    yc = y - mu
    var = _head_sum(yc * yc, e_ref, et_ref) * (1.0 / head)
    yn = yc * lax.rsqrt(var + GN_EPS) * lg_ref[...] + lb_ref[...]
    o_ref[...] = ((yn + bonus_ref[...]) * g_ref[...]).astype(o_ref.dtype)


def _rwkv_post(y, bonus, g, lnx_g, lnx_b, e, et, *, tb):
    n, width = y.shape
    assert n % tb == 0
    row = pl.BlockSpec((tb, width), lambda i: (i, 0))
    full = lambda arr: pl.BlockSpec(arr.shape, lambda i: (0,) * arr.ndim)
    return pl.pallas_call(
        functools.partial(_rwkv_post_kernel, head=RWKV_HEAD),
        grid=(n // tb,),
        in_specs=[row, row, row, full(lnx_g), full(lnx_b), full(e), full(et)],
        out_specs=row,
        out_shape=jax.ShapeDtypeStruct((n, width), BF16),
        compiler_params=_params("parallel"),
    )(y, bonus, g, lnx_g, lnx_b, e, et)


def _to_lanes_prompt(x, batch, seq, heads, split):
    x = x.reshape(batch, seq, heads, RWKV_HEAD).transpose(1, 3, 0, 2)
    x = jnp.broadcast_to(x[..., None], x.shape + (split,))
    return x.reshape(1, seq, RWKV_HEAD, batch * heads * split)


def _rwkv_scan_prompt(r, w, k, a, b, v, batch, seq, heads):
    split = LANES // (batch * heads)
    assert split >= 1 and batch * heads * split == LANES and RWKV_HEAD % (split * SUBLANES) == 0
    vl = RWKV_HEAD // split
    rl, wl, kl, al, bl = (_to_lanes_prompt(t, batch, seq, heads, split) for t in (r, w, k, a, b))
    vv = v.reshape(batch, seq, heads, split, vl).transpose(1, 4, 0, 2, 3).reshape(1, seq, vl, LANES)
    s0 = jnp.zeros((1, RWKV_HEAD, vl, LANES), F32)
    y, s = _rwkv_scan(rl, wl, kl, al, bl, vv, s0, tb=32)
    y = y.reshape(seq, vl, batch, heads, split).transpose(2, 0, 3, 4, 1).reshape(batch * seq, heads * RWKV_HEAD)
    s = s.reshape(RWKV_HEAD, vl, batch, heads, split).transpose(2, 3, 4, 1, 0)
    return y, s.reshape(batch, heads, RWKV_HEAD, RWKV_HEAD)


def _rwkv_scan_sample(r, w, k, a, b, v, s0, n_seq, t_new, heads):
    assert n_seq == LANES
    to = lambda t: t.reshape(n_seq, t_new, heads, RWKV_HEAD).transpose(2, 1, 3, 0)
    s_l = s0.transpose(1, 3, 2, 0)
    y, s = _rwkv_scan(to(r), to(w), to(k), to(a), to(b), to(v), s_l, tb=t_new)
    y = y.transpose(3, 1, 0, 2).reshape(n_seq * t_new, heads * RWKV_HEAD)
    return y, s.transpose(3, 0, 2, 1)


def kernel(x_prompt, x_sample, cache_k, cache_v, state_rwkv, state_shift, page_table, w_ffn1_in, w_ffn1_out, ln1_g, ln1_b, w_in, lambda_q1, lambda_k1, lambda_q2, lambda_k2, subln_g, p_a, mu_shift, w0, w2, a0, a2, g2, k_k, k_a, r_k, lnx_g, lnx_b, p_b, w_out, ln2_g, ln2_b, w_ffn2_in, w_ffn2_out, ln3_g, ln3_b):
    batch, seq, d_model = x_prompt.shape
    n_seq, t_new, _ = x_sample.shape
    depth = w_in.shape[0]
    heads = cache_k.shape[3]
    att_w = heads * ATT_V_DIM
    rw = p_b.shape[1]
    rheads = rw // RWKV_HEAD
    lora_w, lora_a, lora_g = w2.shape[1], a2.shape[1], g2.shape[1]
    shift_w = state_shift.shape[2]
    assert shift_w == 3 * rw + lora_w + lora_a + lora_g and rw == att_w
    alpha = (2 * depth) ** 0.25
    n_p, n_s = batch * seq, n_seq * t_new
    n_tok = n_p + n_s

    col_blk = 1024
    sh_pad = _round_up(shift_w, col_blk)
    g_off = 3 * att_w
    sh_off = g_off + 2 * d_model
    assert sh_off % sh_pad == 0
    lora_g_pad = sh_pad - (3 * rw + lora_w + lora_a)
    src_sh, src_g = 3 * att_w, 3 * att_w + shift_w

    ch = jnp.arange(rw) // RWKV_HEAD
    e = (ch[:, None] == jnp.arange(LANES)[None, :]).astype(F32)
    et = e.T

    n_pool = cache_k.shape[1]
    ck = cache_k.reshape((depth * n_pool,) + cache_k.shape[2:])
    cv = cache_v.reshape((depth * n_pool,) + cache_v.shape[2:])
    x = jnp.concatenate([x_prompt.reshape(n_p, d_model), x_sample.reshape(n_s, d_model)], axis=0)
    xb = x.astype(BF16)
    outs_p, outs_s = [], []
    for l in range(depth):
        lam_init = 0.8 - 0.6 * math.exp(-0.3 * l)
        w_in_l = w_in[l]
        w_in_p = jnp.concatenate(
            [w_in_l[:, :src_sh], w_in_l[:, src_g:], w_in_l[:, src_sh:src_g],
             jnp.zeros((d_model, sh_pad - shift_w), F32)], axis=1).astype(BF16)
        lam_vecs = [t[l].reshape(1, ATT_HEAD_DIM) for t in (lambda_q1, lambda_k1, lambda_q2, lambda_k2)]
        pad_cols = lambda t: jnp.pad(t, ((0, 0), (0, sh_pad - shift_w)))
        lp = dict(mu=pad_cols(mu_shift[l].reshape(1, shift_w)), w0=w0[l].reshape(1, rw), w2=w2[l].astype(BF16),
                  a0=a0[l].reshape(1, rw), a2=a2[l].astype(BF16),
                  g2=jnp.pad(g2[l], ((0, lora_g_pad - lora_g), (0, 0))).astype(BF16),
                  k_k=k_k[l].reshape(1, rw), k_a=k_a[l].reshape(1, rw), r_k=r_k[l].reshape(1, rw))

        x, xb = _swiglu_block(x, xb, w_ffn1_in[l], w_ffn1_out[l].astype(BF16),
                              ln1_g[l], ln1_b[l], alpha)

        (z,) = _matmul(xb, w_in_p, tm=1024, tn=512, epilogue=lambda acc: (acc,), out_dtypes=(F32,))

        o_p = _prompt_attention(z, lam_vecs, subln_g[l], lam_init, batch=batch, seq=seq, heads=heads, tq=512)
        o_s = _sample_attention(z, n_p, ck, cv, l * n_pool, page_table, lam_vecs, subln_g[l], lam_init,
                                heads=heads, t_new=t_new, pages_per_step=4)
        o = jnp.concatenate([o_p, o_s], axis=0)
        (y_a,) = _matmul(o, p_a[l].astype(BF16), tm=1024, tn=512, epilogue=lambda acc: (acc,), out_dtypes=(F32,))

        prep = functools.partial(_rwkv_prep, z, sh_off, sh_pad, lp=lp, e=e, et=et,
                                 width=rw, lora_w=lora_w, lora_a=lora_a)
        init_p = jnp.zeros((batch * SUBLANES, sh_pad), F32)
        init_s = jnp.pad(state_shift[l][:, None, :],
                         ((0, 0), (0, t_new - 1), (0, sh_pad - shift_w))).reshape(n_s, sh_pad)
        r_p, w_p, k_p, v_p, a_p, b_p, g_p, bonus_p = prep(0, n_p, init_p, seq, tb=128)
        r_s, w_s, k_s, v_s, a_s, b_s, g_s, bonus_s = prep(n_p, n_s, init_s, t_new, tb=128)
        y_p, st_p = _rwkv_scan_prompt(r_p, w_p, k_p, a_p, b_p, v_p, batch, seq, rheads)
        y_s, st_s = _rwkv_scan_sample(r_s, w_s, k_s, a_s, b_s, v_s, state_rwkv[l], n_seq, t_new, rheads)
        cat = lambda p, s: jnp.concatenate([p, s], axis=0)
        yb = _rwkv_post(cat(y_p, y_s), cat(bonus_p, bonus_s), cat(g_p, g_s),
                        lnx_g[l].reshape(1, rw), lnx_b[l].reshape(1, rw), e, et, tb=256)

        gblk = g_off // 512
        (merged,) = _matmul(
            yb, p_b[l].astype(BF16), tm=1024, tn=512,
            epilogue=lambda acc, ya, ga, gb: (jax.nn.sigmoid(ga) * ya + jax.nn.sigmoid(gb) * acc,),
            out_dtypes=(BF16,), extras=((y_a, 0), (z, gblk), (z, gblk + d_model // 512)))
        (pre,) = _matmul(merged, w_out[l].astype(BF16), tm=1024, tn=512,
                         epilogue=lambda acc, xr: (alpha * xr + acc,), out_dtypes=(F32,), extras=((x, 0),))
        x, xb = _layer_norm(pre, ln2_g[l], ln2_b[l], tr=256)

        x, xb = _swiglu_block(x, xb, w_ffn2_in[l], w_ffn2_out[l].astype(BF16),
                              ln3_g[l], ln3_b[l], alpha)

        kv = lambda lo, n0, n1, shape: z[n0:n1, lo:lo + att_w].reshape(shape)
        outs_p.append((kv(att_w, 0, n_p, (batch, seq, heads, ATT_V_DIM)),
                       kv(2 * att_w, 0, n_p, (batch, seq, heads, ATT_V_DIM)),
                       st_p, z[seq - 1:n_p:seq, sh_off:sh_off + shift_w]))
        outs_s.append((kv(att_w, n_p, n_tok, (n_seq, t_new, heads, ATT_V_DIM)),
                       kv(2 * att_w, n_p, n_tok, (n_seq, t_new, heads, ATT_V_DIM)),
                       st_s, z[n_p + t_new - 1:n_tok:t_new, sh_off:sh_off + shift_w]))

    k_p, v_p, s_p, sh_p = (jnp.stack(t) for t in zip(*outs_p))
    k_s, v_s, s_s, sh_s = (jnp.stack(t) for t in zip(*outs_s))
    return (x[:n_p].reshape(batch, seq, d_model), x[n_p:].reshape(n_seq, t_new, d_model),
            k_p, v_p, s_p, sh_p, k_s, v_s, s_s, sh_s)
```

```python
import functools
import math

import jax
import jax.numpy as jnp
from jax import lax
from jax.experimental import pallas as pl
from jax.experimental.pallas import tpu as pltpu

F32 = jnp.float32
BF16 = jnp.bfloat16

LANES = 128
SUBLANES = 8
MXU_DIM = 256
VMEM_LIMIT_BYTES = 56 * 1024 * 1024

ATT_HEAD_DIM = 128
ATT_V_DIM = 2 * ATT_HEAD_DIM
ATT_SCALE = ATT_HEAD_DIM ** -0.5
NEG_INF = -1e30
RMS_EPS = 1e-5
RWKV_HEAD = 64
GN_EPS = 64e-5
LN_EPS = 1e-5


def _round_up(x, m):
    return (x + m - 1) // m * m


def _params(*semantics):
    return pltpu.CompilerParams(dimension_semantics=semantics,
                                vmem_limit_bytes=VMEM_LIMIT_BYTES)


def _mm_kernel(*refs, n_extra, epilogue):
    x_ref, w_ref = refs[0], refs[1]
    extras = refs[2:2 + n_extra]
    outs = refs[2 + n_extra:]
    acc = jnp.dot(x_ref[...].astype(BF16), w_ref[...], preferred_element_type=F32)
    res = epilogue(acc, *[e[...] for e in extras])
    for o_ref, r in zip(outs, res):
        o_ref[...] = r.astype(o_ref.dtype)


def _matmul(x, w, *, tm, tn, epilogue, out_dtypes, extras=(), n_cols=None):
    m, k = x.shape
    n = w.shape[1] if n_cols is None else n_cols
    assert m % tm == 0 and n % tn == 0 and w.shape[0] == k
    in_specs = [pl.BlockSpec((tm, k), lambda i, j: (i, 0)),
                pl.BlockSpec((k, tn), lambda i, j: (0, j))]
    args = [x, w]
    for arr, off in extras:
        in_specs.append(pl.BlockSpec((tm, tn), lambda i, j, off=off: (i, j + off)))
        args.append(arr)
    return pl.pallas_call(
        functools.partial(_mm_kernel, n_extra=len(extras), epilogue=epilogue),
        grid=(m // tm, n // tn),
        in_specs=in_specs,
        out_specs=[pl.BlockSpec((tm, tn), lambda i, j: (i, j)) for _ in out_dtypes],
        out_shape=[jax.ShapeDtypeStruct((m, n), dt) for dt in out_dtypes],
        compiler_params=_params("parallel", "arbitrary"),
    )(*args)


def _ffn_in_kernel(x_ref, wg_ref, wu_ref, h_ref):
    x = x_ref[...]
    g = jnp.dot(x, wg_ref[...].astype(BF16), preferred_element_type=F32)
    u = jnp.dot(x, wu_ref[...].astype(BF16), preferred_element_type=F32)
    h_ref[...] = (g * jax.nn.sigmoid(g) * u).astype(h_ref.dtype)


def _ffn_in(x, w, *, tm, tn):
    m, d = x.shape
    f = w.shape[1] // 2
    assert m % tm == 0 and f % tn == 0
    nf = f // tn
    return pl.pallas_call(
        _ffn_in_kernel,
        grid=(m // tm, nf),
        in_specs=[pl.BlockSpec((tm, d), lambda i, j: (i, 0)),
                  pl.BlockSpec((d, tn), lambda i, j: (0, j)),
                  pl.BlockSpec((d, tn), lambda i, j: (0, j + nf))],
        out_specs=pl.BlockSpec((tm, tn), lambda i, j: (i, j)),
        out_shape=jax.ShapeDtypeStruct((m, f), BF16),
        compiler_params=_params("parallel", "arbitrary"),
    )(x, w, w)


def _ln_kernel(x_ref, g_ref, b_ref, o_ref, ob_ref):
    x = x_ref[...]
    mu = jnp.mean(x, axis=-1, keepdims=True)
    xc = x - mu
    var = jnp.mean(xc * xc, axis=-1, keepdims=True)
    y = xc * lax.rsqrt(var + LN_EPS) * g_ref[...] + b_ref[...]
    o_ref[...] = y
    ob_ref[...] = y.astype(BF16)


def _layer_norm(x, g, b, *, tr):
    m, d = x.shape
    assert m % tr == 0
    row = pl.BlockSpec((tr, d), lambda i: (i, 0))
    vec = pl.BlockSpec((1, d), lambda i: (0, 0))
    return pl.pallas_call(
        _ln_kernel,
        grid=(m // tr,),
        in_specs=[row, vec, vec],
        out_specs=[row, row],
        out_shape=[jax.ShapeDtypeStruct((m, d), F32), jax.ShapeDtypeStruct((m, d), BF16)],
        compiler_params=_params("parallel"),
    )(x, g.reshape(1, d), b.reshape(1, d))


def _swiglu_block(x, xb, w_in, w_out_b, g, b, alpha):
    h = _ffn_in(xb, w_in, tm=1024, tn=256)
    (pre,) = _matmul(h, w_out_b, tm=512, tn=256,
                     epilogue=lambda acc, xr: (alpha * xr + 0.5 * acc,),
                     out_dtypes=(F32,), extras=((x, 0),))
    return _layer_norm(pre, g, b, tr=256)


EXP2_SCALE = ATT_SCALE * math.log2(math.e)


def _lambda_value(lq1, lk1, lq2, lk2, lam_init):
    return (jnp.exp(jnp.sum(lq1[...] * lk1[...], keepdims=True))
            - jnp.exp(jnp.sum(lq2[...] * lk2[...], keepdims=True)) + lam_init)


def _sub_norm(o, g, lam_init):
    ms = jnp.mean(o * o, axis=-1, keepdims=True)
    return o * lax.rsqrt(ms + RMS_EPS) * g * (1.0 - lam_init)


def _online_softmax_step(s, v, m_ref, l_ref, acc_ref):
    m_prev = m_ref[...]
    m_new = jnp.maximum(m_prev, jnp.max(s, axis=-1, keepdims=True))
    p = jnp.exp2((s - m_new) * EXP2_SCALE)
    corr = jnp.exp2((m_prev - m_new) * EXP2_SCALE)
    l_ref[...] = corr * l_ref[...] + jnp.sum(p, axis=-1, keepdims=True)
    acc_ref[...] = corr * acc_ref[...] + jnp.dot(p.astype(BF16), v, preferred_element_type=F32)
    m_ref[...] = m_new


def _prompt_attn_kernel(lq1, lk1, lq2, lk2, g_ref, q_ref, k_ref, v_ref, o_ref,
                        kb, vb, m1, l1, a1, m2, l2, a2, *, tq, lam_init):
    qi = pl.program_id(2)
    branches = ((0, m1, l1, a1), (ATT_HEAD_DIM, m2, l2, a2))

    @pl.when(qi == 0)
    def _():
        kb[...] = k_ref[...].astype(BF16)
        vb[...] = v_ref[...].astype(BF16)

    for _, m_ref, l_ref, a_ref in branches:
        m_ref[...] = jnp.full(m_ref.shape, NEG_INF, F32)
        l_ref[...] = jnp.zeros(l_ref.shape, F32)
        a_ref[...] = jnp.zeros(a_ref.shape, F32)
    q = q_ref[...].astype(BF16)

    def kv_block(ki, causal):
        start = pl.multiple_of(ki * tq, tq)
        k = kb[pl.ds(start, tq), :]
        v = vb[pl.ds(start, tq), :]
        for lo, m_ref, l_ref, a_ref in branches:
            s = lax.dot_general(q[:, lo:lo + ATT_HEAD_DIM], k[:, lo:lo + ATT_HEAD_DIM],
                                (((1,), (1,)), ((), ())), preferred_element_type=F32)
            if causal:
                rows = lax.broadcasted_iota(jnp.int32, (tq, tq), 0)
                cols = lax.broadcasted_iota(jnp.int32, (tq, tq), 1)
                s = jnp.where(cols <= rows, s, NEG_INF)
            _online_softmax_step(s, v, m_ref, l_ref, a_ref)

    def body(ki, carry):
        kv_block(ki, False)
        return carry

    lax.fori_loop(0, qi, body, 0)
    kv_block(qi, True)

    lam = _lambda_value(lq1, lk1, lq2, lk2, lam_init)
    o = a1[...] / l1[...] - lam * (a2[...] / l2[...])
    o_ref[...] = _sub_norm(o, g_ref[...], lam_init).astype(o_ref.dtype)


def _prompt_attention(z, lam_vecs, subln_g, lam_init, *, batch, seq, heads, tq):
    nq = seq // tq
    kv_off = heads
    vec = pl.BlockSpec((1, ATT_HEAD_DIM), lambda b, h, qi: (0, 0))
    blk = (tq, ATT_V_DIM)
    full = (seq, ATT_V_DIM)
    return pl.pallas_call(
        functools.partial(_prompt_attn_kernel, tq=tq, lam_init=lam_init),
        grid=(batch, heads, nq),
        in_specs=[vec, vec, vec, vec,
                  pl.BlockSpec((1, ATT_V_DIM), lambda b, h, qi: (0, 0)),
                  pl.BlockSpec(blk, lambda b, h, qi: (b * nq + qi, h)),
                  pl.BlockSpec(full, lambda b, h, qi: (b, kv_off + h)),
                  pl.BlockSpec(full, lambda b, h, qi: (b, 2 * kv_off + h))],
        out_specs=pl.BlockSpec(blk, lambda b, h, qi: (b * nq + qi, h)),
        out_shape=jax.ShapeDtypeStruct((batch * seq, heads * ATT_V_DIM), F32),
        scratch_shapes=[pltpu.VMEM(full, BF16), pltpu.VMEM(full, BF16),
                        pltpu.VMEM((tq, 1), F32), pltpu.VMEM((tq, 1), F32), pltpu.VMEM((tq, ATT_V_DIM), F32),
                        pltpu.VMEM((tq, 1), F32), pltpu.VMEM((tq, 1), F32), pltpu.VMEM((tq, ATT_V_DIM), F32)],
        compiler_params=_params("parallel", "parallel", "arbitrary"),
    )(*lam_vecs, subln_g.reshape(1, ATT_V_DIM), z, z, z)


def _sample_attn_kernel(*refs, n_pages, heads, t_new, page, lam_init):
    pt_ref = refs[0]
    lq1, lk1, lq2, lk2, g_ref, q_ref, kn_ref, vn_ref = refs[1:9]
    k_refs = refs[9:9 + n_pages]
    v_refs = refs[9 + n_pages:9 + 2 * n_pages]
    o_ref = refs[9 + 2 * n_pages]
    qt, kb, vb, m_ref, l_ref, acc_ref = refs[10 + 2 * n_pages:]
    del pt_ref
    j = pl.program_id(1)
    rows = 2 * heads * t_new
    prow = page * heads

    @pl.when(j == 0)
    def _():
        q = q_ref[...]
        lane = lax.broadcasted_iota(jnp.int32, (t_new, ATT_V_DIM), 1)
        pieces = []
        for br in range(2):
            keep = (lane >= ATT_HEAD_DIM) if br else (lane < ATT_HEAD_DIM)
            for h in range(heads):
                pieces.append(jnp.where(keep, q[:, h * ATT_V_DIM:(h + 1) * ATT_V_DIM], 0.0))
        qt[...] = jnp.concatenate(pieces, axis=0).astype(BF16)
        m_ref[...] = jnp.full(m_ref.shape, NEG_INF, F32)
        l_ref[...] = jnp.zeros(l_ref.shape, F32)
        acc_ref[...] = jnp.zeros(acc_ref.shape, F32)

    def scores(keys, n_keys, causal):
        s = lax.dot_general(keys, qt[...], (((1,), (1,)), ((), ())), preferred_element_type=F32)
        s = s.T
        r_idx = lax.broadcasted_iota(jnp.int32, (rows, n_keys * heads), 0)
        c_idx = lax.broadcasted_iota(jnp.int32, (rows, n_keys * heads), 1)
        keep = (c_idx % heads) == ((r_idx // t_new) % heads)
        if causal:
            keep = keep & ((c_idx // heads) <= (r_idx % t_new))
        return jnp.where(keep, s, NEG_INF)

    for i in range(n_pages):
        kb[i * prow:(i + 1) * prow, :] = k_refs[i][...].reshape(prow, ATT_V_DIM).astype(BF16)
        vb[i * prow:(i + 1) * prow, :] = v_refs[i][...].reshape(prow, ATT_V_DIM).astype(BF16)
    _online_softmax_step(scores(kb[...], n_pages * page, False), vb[...], m_ref, l_ref, acc_ref)

    @pl.when(j == pl.num_programs(1) - 1)
    def _():
        def rows_of(ref):
            x = ref[...].reshape(t_new, heads, ATT_V_DIM).reshape(t_new * heads, ATT_V_DIM)
            pad = jnp.zeros((LANES - t_new * heads, ATT_V_DIM), F32)
            return jnp.concatenate([x, pad], axis=0).astype(BF16)
        _online_softmax_step(scores(rows_of(kn_ref), LANES // heads, True), rows_of(vn_ref),
                             m_ref, l_ref, acc_ref)
        lam = _lambda_value(lq1, lk1, lq2, lk2, lam_init)
        o = acc_ref[...] / l_ref[...]
        half = heads * t_new
        for h in range(heads):
            oh = o[h * t_new:(h + 1) * t_new, :] - lam * o[half + h * t_new:half + (h + 1) * t_new, :]
            o_ref[:, h * ATT_V_DIM:(h + 1) * ATT_V_DIM] = _sub_norm(oh, g_ref[...], lam_init)


def _sample_attention(z, row0, cache_k, cache_v, page0, page_table, lam_vecs, subln_g, lam_init,
                      *, heads, t_new, pages_per_step):
    n_seq, n_logical = page_table.shape
    page = cache_k.shape[1]
    width = heads * ATT_V_DIM
    rows = 2 * heads * t_new
    assert rows == LANES and t_new == SUBLANES and n_logical % pages_per_step == 0
    assert row0 % t_new == 0
    nsteps = n_logical // pages_per_step
    rb0 = row0 // t_new
    vec = pl.BlockSpec((1, ATT_HEAD_DIM), lambda b, j, pt: (0, 0))

    def page_spec(i):
        return pl.BlockSpec(
            (None, page, heads, ATT_V_DIM),
            lambda b, j, pt, i=i: (page0 + pt[b * n_logical + j * pages_per_step + i], 0, 0, 0))

    new_spec = lambda c: pl.BlockSpec((t_new, width), lambda b, j, pt, c=c: (rb0 + b, c))
    key_rows = pages_per_step * page * heads
    grid_spec = pltpu.PrefetchScalarGridSpec(
        num_scalar_prefetch=1,
        grid=(n_seq, nsteps),
        in_specs=[vec, vec, vec, vec,
                  pl.BlockSpec((1, ATT_V_DIM), lambda b, j, pt: (0, 0)),
                  new_spec(0), new_spec(1), new_spec(2)]
                 + [page_spec(i) for i in range(pages_per_step)]
                 + [page_spec(i) for i in range(pages_per_step)],
        out_specs=pl.BlockSpec((t_new, width), lambda b, j, pt: (b, 0)),
        scratch_shapes=[pltpu.VMEM((rows, ATT_V_DIM), BF16),
                        pltpu.VMEM((key_rows, ATT_V_DIM), BF16),
                        pltpu.VMEM((key_rows, ATT_V_DIM), BF16),
                        pltpu.VMEM((rows, 1), F32), pltpu.VMEM((rows, 1), F32),
                        pltpu.VMEM((rows, ATT_V_DIM), F32)])
    return pl.pallas_call(
        functools.partial(_sample_attn_kernel, n_pages=pages_per_step, heads=heads,
                          t_new=t_new, page=page, lam_init=lam_init),
        grid_spec=grid_spec,
        out_shape=jax.ShapeDtypeStruct((n_seq * t_new, width), F32),
        compiler_params=_params("parallel", "arbitrary"),
    )(page_table.reshape(-1), *lam_vecs, subln_g.reshape(1, ATT_V_DIM), z, z, z,
      *([cache_k] * pages_per_step), *([cache_v] * pages_per_step))


def _dot_indicator(x, ind):
    x1 = x.astype(BF16)
    r1 = x - x1.astype(F32)
    x2 = r1.astype(BF16)
    x3 = (r1 - x2.astype(F32)).astype(BF16)
    return (jnp.dot(x1, ind, preferred_element_type=F32) + jnp.dot(x2, ind, preferred_element_type=F32)
            + jnp.dot(x3, ind, preferred_element_type=F32))


def _head_sum(x, e_ref, et_ref):
    return _dot_indicator(_dot_indicator(x, e_ref[...]), et_ref[...])


def _rwkv_prep_kernel(zs_ref, prev_ref, init_ref, mu_ref, w0_ref, w2_ref, a0_ref, a2_ref, g2_ref,
                      kk_ref, ka_ref, rk_ref, e_ref, et_ref,
                      r_out, w_out, k_out, v_out, a_out, b_out, g_out, bonus_out,
                      *, tb, seq_len, width, lora_w, lora_a):
    zs = zs_ref[...]
    row = lax.broadcasted_iota(jnp.int32, (tb, 1), 0)
    shifted = pltpu.roll(zs, 1, 0)
    if seq_len >= tb:
        at_start = (pl.program_id(0) * tb) % seq_len == 0
        carry = jnp.where(at_start, init_ref[0:1, :], prev_ref[SUBLANES - 1:SUBLANES, :])
        z_prev = jnp.where(row == 0, carry, shifted)
    else:
        z_prev = jnp.where(row % seq_len == 0, init_ref[...], shifted)
    zm = zs + (z_prev - zs) * mu_ref[...]
    r = zm[:, 0:width]
    kr = zm[:, width:2 * width]
    vr = zm[:, 2 * width:3 * width]
    o = 3 * width
    wd = zm[:, o:o + lora_w]
    ad = zm[:, o + lora_w:o + lora_w + lora_a]
    gd = zm[:, o + lora_w + lora_a:]
    wl = w0_ref[...] + jnp.dot(jnp.tanh(wd).astype(BF16), w2_ref[...], preferred_element_type=F32)
    softplus = jnp.maximum(-wl, 0.0) + jnp.log1p(jnp.exp(-jnp.abs(wl)))
    w_log = -softplus - 0.5
    decay = jnp.exp(-jnp.exp(w_log))
    a = jax.nn.sigmoid(a0_ref[...] + jnp.dot(ad.astype(BF16), a2_ref[...], preferred_element_type=F32))
    g = jnp.dot(jax.nn.sigmoid(gd).astype(BF16), g2_ref[...], preferred_element_type=F32)
    kk = kr * kk_ref[...]
    norm = jnp.sqrt(_head_sum(kk * kk, e_ref, et_ref))
    kk = kk / jnp.maximum(norm, 1e-12)
    kmod = kr * (1.0 + (a - 1.0) * ka_ref[...])
    bonus = _head_sum(r * kmod * rk_ref[...], e_ref, et_ref) * vr
    r_out[...] = r
    w_out[...] = decay
    k_out[...] = kmod
    v_out[...] = vr
    a_out[...] = -kk
    b_out[...] = kk * a
    g_out[...] = g
    bonus_out[...] = bonus


def _rwkv_prep(z, col0, zs_cols, row0, n_rows, init_rows, seq_len, lp, e, et, *, tb, width, lora_w, lora_a):
    assert n_rows % tb == 0 and row0 % tb == 0 and col0 % zs_cols == 0 and tb % SUBLANES == 0
    cb = col0 // zs_cols
    rb0 = row0 // tb
    nb = n_rows // tb
    per8 = tb // SUBLANES
    zs_spec = pl.BlockSpec((tb, zs_cols), lambda i: (rb0 + i, cb))
    prev_spec = pl.BlockSpec((SUBLANES, zs_cols),
                             lambda i: (jnp.maximum((rb0 + i) * per8 - 1, 0), cb))
    if seq_len >= tb:
        assert seq_len % tb == 0
        spb = seq_len // tb
        init_spec = pl.BlockSpec((SUBLANES, zs_cols), lambda i: (i // spb, 0))
    else:
        assert tb % seq_len == 0
        init_spec = pl.BlockSpec((tb, zs_cols), lambda i: (i, 0))
    full = lambda arr: pl.BlockSpec(arr.shape, lambda i: (0,) * arr.ndim)
    consts = [lp['mu'], lp['w0'], lp['w2'], lp['a0'], lp['a2'], lp['g2'],
              lp['k_k'], lp['k_a'], lp['r_k'], e, et]
    out_spec = pl.BlockSpec((tb, width), lambda i: (i, 0))
    return pl.pallas_call(
        functools.partial(_rwkv_prep_kernel, tb=tb, seq_len=seq_len, width=width,
                          lora_w=lora_w, lora_a=lora_a),
        grid=(nb,),
        in_specs=[zs_spec, prev_spec, init_spec] + [full(c) for c in consts],
        out_specs=[out_spec] * 8,
        out_shape=[jax.ShapeDtypeStruct((n_rows, width), F32)] * 8,
        compiler_params=_params("parallel"),
    )(z, z, init_rows, *consts)


def _rwkv_scan_kernel(r_ref, w_ref, k_ref, a_ref, an_ref, b_ref, v_ref, s0_ref, y_ref, sout_ref,
                      s_scr, sa_scr, *, tb, n_key, n_val):
    t_blk = pl.program_id(1)
    nv = n_val // SUBLANES
    vrows = [slice(j * SUBLANES, (j + 1) * SUBLANES) for j in range(nv)]

    def bcast(ref, t, c):
        return jnp.broadcast_to(ref[0, t, pl.ds(c, 1), :], (SUBLANES, LANES))

    @pl.when(t_blk == 0)
    def _():
        s_scr[...] = s0_ref[0]
        sa0 = [jnp.zeros((SUBLANES, LANES), F32) for _ in range(nv)]
        for c in range(n_key):
            a_c = bcast(a_ref, 0, c)
            for j in range(nv):
                sa0[j] = sa0[j] + s0_ref[0, c, vrows[j], :] * a_c
        for j in range(nv):
            sa_scr[vrows[j], :] = sa0[j]

    def token(t, sa):
        last = t == tb - 1
        t_next = jnp.minimum(t + 1, tb - 1)
        vt = [v_ref[0, t, vrows[j], :] for j in range(nv)]
        y = [jnp.zeros((SUBLANES, LANES), F32) for _ in range(nv)]
        sa_next = [jnp.zeros((SUBLANES, LANES), F32) for _ in range(nv)]
        for c in range(n_key):
            w_c = bcast(w_ref, t, c)
            b_c = bcast(b_ref, t, c)
            k_c = bcast(k_ref, t, c)
            r_c = bcast(r_ref, t, c)
            a_c = jnp.where(last, bcast(an_ref, 0, c), bcast(a_ref, t_next, c))
            for j in range(nv):
                s_new = s_scr[c, vrows[j], :] * w_c + sa[j] * b_c + vt[j] * k_c
                s_scr[c, vrows[j], :] = s_new
                y[j] = y[j] + s_new * r_c
                sa_next[j] = sa_next[j] + s_new * a_c
        for j in range(nv):
            y_ref[0, t, vrows[j], :] = y[j]
        return tuple(sa_next)

    sa_end = lax.fori_loop(0, tb, token, tuple(sa_scr[vrows[j], :] for j in range(nv)))
    for j in range(nv):
        sa_scr[vrows[j], :] = sa_end[j]

    @pl.when(t_blk == pl.num_programs(1) - 1)
    def _():
        sout_ref[0] = s_scr[...]


def _rwkv_scan(r, w, k, a, b, v, s0, *, tb):
    groups, t_len, n_key, _ = r.shape
    n_val = v.shape[2]
    assert t_len % tb == 0
    nt = t_len // tb
    kspec = pl.BlockSpec((1, tb, n_key, LANES), lambda g, t: (g, t, 0, 0))
    nspec = pl.BlockSpec((1, tb, n_key, LANES), lambda g, t: (g, jnp.minimum(t + 1, nt - 1), 0, 0))
    vspec = pl.BlockSpec((1, tb, n_val, LANES), lambda g, t: (g, t, 0, 0))
    sspec = pl.BlockSpec((1, n_key, n_val, LANES), lambda g, t: (g, 0, 0, 0))
    return pl.pallas_call(
        functools.partial(_rwkv_scan_kernel, tb=tb, n_key=n_key, n_val=n_val),
        grid=(groups, nt),
        in_specs=[kspec] * 4 + [nspec, kspec, vspec, sspec],
        out_specs=[vspec, sspec],
        out_shape=[jax.ShapeDtypeStruct(v.shape, F32), jax.ShapeDtypeStruct(s0.shape, F32)],
        scratch_shapes=[pltpu.VMEM((n_key, n_val, LANES), F32), pltpu.VMEM((n_val, LANES), F32)],
        compiler_params=_params("parallel", "arbitrary"),
    )(r, w, k, a, a, b, v, s0)


def _rwkv_post_kernel(y_ref, bonus_ref, g_ref, lg_ref, lb_ref, e_ref, et_ref, o_ref, *, head):
    y = y_ref[...]
    mu = _head_sum(y, e_ref, et_ref) * (1.0 / head)
    yc = y - mu
    var = _head_sum(yc * yc, e_ref, et_ref) * (1.0 / head)
    yn = yc * lax.rsqrt(var + GN_EPS) * lg_ref[...] + lb_ref[...]
    o_ref[...] = ((yn + bonus_ref[...]) * g_ref[...]).astype(o_ref.dtype)


def _rwkv_post(y, bonus, g, lnx_g, lnx_b, e, et, *, tb):
    n, width = y.shape
    assert n % tb == 0
    row = pl.BlockSpec((tb, width), lambda i: (i, 0))
    full = lambda arr: pl.BlockSpec(arr.shape, lambda i: (0,) * arr.ndim)
    return pl.pallas_call(
        functools.partial(_rwkv_post_kernel, head=RWKV_HEAD),
        grid=(n // tb,),
        in_specs=[row, row, row, full(lnx_g), full(lnx_b), full(e), full(et)],
        out_specs=row,
        out_shape=jax.ShapeDtypeStruct((n, width), BF16),
        compiler_params=_params("parallel"),
    )(y, bonus, g, lnx_g, lnx_b, e, et)


def _to_lanes_kernel(*refs, batch, heads, split, tb, n_key):
    n_in = (n_key + 1) * batch
    ins, outs = refs[:n_in], refs[n_in:]
    vl = RWKV_HEAD // split

    def per_head(ref):
        return ref[...].reshape(tb, heads, RWKV_HEAD)

    for i in range(n_key):
        parts = []
        for b in range(batch):
            parts += [per_head(ins[i * batch + b])] * split
        outs[i][...] = jnp.swapaxes(jnp.concatenate(parts, axis=1), 1, 2)
    parts = []
    for b in range(batch):
        x = per_head(ins[n_key * batch + b])
        parts += [x[:, :, s * vl:(s + 1) * vl] for s in range(split)]
    outs[n_key][...] = jnp.swapaxes(jnp.concatenate(parts, axis=1), 1, 2)


def _to_lanes(key_arrays, v, *, batch, seq, heads, split, tb):
    width = heads * RWKV_HEAD
    n_key = len(key_arrays)
    nblk = seq // tb
    assert seq % tb == 0 and batch * split * heads == LANES
    vl = RWKV_HEAD // split
    in_specs, args = [], []
    for arr in list(key_arrays) + [v]:
        for b in range(batch):
            in_specs.append(pl.BlockSpec((tb, width), lambda i, b=b: (b * nblk + i, 0)))
            args.append(arr)
    kout = pl.BlockSpec((tb, RWKV_HEAD, LANES), lambda i: (i, 0, 0))
    vout = pl.BlockSpec((tb, vl, LANES), lambda i: (i, 0, 0))
    return pl.pallas_call(
        functools.partial(_to_lanes_kernel, batch=batch, heads=heads, split=split, tb=tb, n_key=n_key),
        grid=(nblk,),
        in_specs=in_specs,
        out_specs=[kout] * n_key + [vout],
        out_shape=[jax.ShapeDtypeStruct((seq, RWKV_HEAD, LANES), F32)] * n_key
                  + [jax.ShapeDtypeStruct((seq, vl, LANES), F32)],
        compiler_params=_params("parallel"),
    )(*args)


def _from_lanes_kernel(y_ref, o_ref, *, batch, heads, split, tb):
    yt = jnp.swapaxes(y_ref[...], 1, 2)
    for b in range(batch):
        pieces = [yt[:, (b * split + s) * heads:(b * split + s + 1) * heads, :] for s in range(split)]
        o_ref[b] = jnp.concatenate(pieces, axis=2).reshape(tb, heads * RWKV_HEAD)


def _from_lanes(y, *, batch, seq, heads, split, tb):
    vl = RWKV_HEAD // split
    return pl.pallas_call(
        functools.partial(_from_lanes_kernel, batch=batch, heads=heads, split=split, tb=tb),
        grid=(seq // tb,),
        in_specs=[pl.BlockSpec((tb, vl, LANES), lambda i: (i, 0, 0))],
        out_specs=pl.BlockSpec((batch, tb, heads * RWKV_HEAD), lambda i: (0, i, 0)),
        out_shape=jax.ShapeDtypeStruct((batch, seq, heads * RWKV_HEAD), F32),
        compiler_params=_params("parallel"),
    )(y)


def _rwkv_scan_prompt(r, w, k, a, b, v, batch, seq, heads):
    split = LANES // (batch * heads)
    assert split >= 1 and batch * heads * split == LANES and RWKV_HEAD % (split * SUBLANES) == 0
    vl = RWKV_HEAD // split
    geom = dict(batch=batch, seq=seq, heads=heads, split=split, tb=32)
    rl, wl, kl, al, bl, vv = (t[None] for t in _to_lanes((r, w, k, a, b), v, **geom))
    s0 = jnp.zeros((1, RWKV_HEAD, vl, LANES), F32)
    y, s = _rwkv_scan(rl, wl, kl, al, bl, vv, s0, tb=32)
    y = _from_lanes(y[0], **geom).reshape(batch * seq, heads * RWKV_HEAD)
    s = s.reshape(RWKV_HEAD, vl, batch, split, heads).transpose(2, 4, 3, 1, 0)
    return y, s.reshape(batch, heads, RWKV_HEAD, RWKV_HEAD)


def _rwkv_scan_sample(r, w, k, a, b, v, s0, n_seq, t_new, heads):
    assert n_seq == LANES
    to = lambda t: t.reshape(n_seq, t_new, heads, RWKV_HEAD).transpose(2, 1, 3, 0)
    s_l = s0.transpose(1, 3, 2, 0)
    y, s = _rwkv_scan(to(r), to(w), to(k), to(a), to(b), to(v), s_l, tb=t_new)
    y = y.transpose(3, 1, 0, 2).reshape(n_seq * t_new, heads * RWKV_HEAD)
    return y, s.transpose(3, 0, 2, 1)


def kernel(x_prompt, x_sample, cache_k, cache_v, state_rwkv, state_shift, page_table, w_ffn1_in, w_ffn1_out, ln1_g, ln1_b, w_in, lambda_q1, lambda_k1, lambda_q2, lambda_k2, subln_g, p_a, mu_shift, w0, w2, a0, a2, g2, k_k, k_a, r_k, lnx_g, lnx_b, p_b, w_out, ln2_g, ln2_b, w_ffn2_in, w_ffn2_out, ln3_g, ln3_b):
    batch, seq, d_model = x_prompt.shape
    n_seq, t_new, _ = x_sample.shape
    depth = w_in.shape[0]
    heads = cache_k.shape[3]
    att_w = heads * ATT_V_DIM
    rw = p_b.shape[1]
    rheads = rw // RWKV_HEAD
    lora_w, lora_a, lora_g = w2.shape[1], a2.shape[1], g2.shape[1]
    shift_w = state_shift.shape[2]
    assert shift_w == 3 * rw + lora_w + lora_a + lora_g and rw == att_w
    alpha = (2 * depth) ** 0.25
    n_p, n_s = batch * seq, n_seq * t_new
    n_tok = n_p + n_s

    col_blk = 1024
    sh_pad = _round_up(shift_w, col_blk)
    g_off = 3 * att_w
    sh_off = g_off + 2 * d_model
    assert sh_off % sh_pad == 0
    lora_g_pad = sh_pad - (3 * rw + lora_w + lora_a)
    src_sh, src_g = 3 * att_w, 3 * att_w + shift_w

    ch = jnp.arange(rw) // RWKV_HEAD
    e = (ch[:, None] == jnp.arange(LANES)[None, :]).astype(BF16)
    et = e.T

    n_pool = cache_k.shape[1]
    ck = cache_k.reshape((depth * n_pool,) + cache_k.shape[2:])
    cv = cache_v.reshape((depth * n_pool,) + cache_v.shape[2:])
    x = jnp.concatenate([x_prompt.reshape(n_p, d_model), x_sample.reshape(n_s, d_model)], axis=0)
    xb = x.astype(BF16)
    outs_p, outs_s = [], []
    for l in range(depth):
        lam_init = 0.8 - 0.6 * math.exp(-0.3 * l)
        w_in_l = w_in[l]
        w_in_p = jnp.concatenate(
            [w_in_l[:, :src_sh], w_in_l[:, src_g:], w_in_l[:, src_sh:src_g],
             jnp.zeros((d_model, sh_pad - shift_w), F32)], axis=1).astype(BF16)
        lam_vecs = [t[l].reshape(1, ATT_HEAD_DIM) for t in (lambda_q1, lambda_k1, lambda_q2, lambda_k2)]
        pad_cols = lambda t: jnp.pad(t, ((0, 0), (0, sh_pad - shift_w)))
        lp = dict(mu=pad_cols(mu_shift[l].reshape(1, shift_w)), w0=w0[l].reshape(1, rw), w2=w2[l].astype(BF16),
                  a0=a0[l].reshape(1, rw), a2=a2[l].astype(BF16),
                  g2=jnp.pad(g2[l], ((0, lora_g_pad - lora_g), (0, 0))).astype(BF16),
                  k_k=k_k[l].reshape(1, rw), k_a=k_a[l].reshape(1, rw), r_k=r_k[l].reshape(1, rw))

        x, xb = _swiglu_block(x, xb, w_ffn1_in[l], w_ffn1_out[l].astype(BF16),
                              ln1_g[l], ln1_b[l], alpha)

        (z,) = _matmul(xb, w_in_p, tm=1024, tn=512, epilogue=lambda acc: (acc,), out_dtypes=(F32,))

        o_p = _prompt_attention(z, lam_vecs, subln_g[l], lam_init, batch=batch, seq=seq, heads=heads, tq=512)
        o_s = _sample_attention(z, n_p, ck, cv, l * n_pool, page_table, lam_vecs, subln_g[l], lam_init,
                                heads=heads, t_new=t_new, pages_per_step=4)
        o = jnp.concatenate([o_p, o_s], axis=0)
        (y_a,) = _matmul(o, p_a[l].astype(BF16), tm=1024, tn=512, epilogue=lambda acc: (acc,), out_dtypes=(F32,))

        prep = functools.partial(_rwkv_prep, z, sh_off, sh_pad, lp=lp, e=e, et=et,
                                 width=rw, lora_w=lora_w, lora_a=lora_a)
        init_p = jnp.zeros((batch * SUBLANES, sh_pad), F32)
        init_s = jnp.pad(state_shift[l][:, None, :],
                         ((0, 0), (0, t_new - 1), (0, sh_pad - shift_w))).reshape(n_s, sh_pad)
        r_p, w_p, k_p, v_p, a_p, b_p, g_p, bonus_p = prep(0, n_p, init_p, seq, tb=128)
        r_s, w_s, k_s, v_s, a_s, b_s, g_s, bonus_s = prep(n_p, n_s, init_s, t_new, tb=128)
        y_p, st_p = _rwkv_scan_prompt(r_p, w_p, k_p, a_p, b_p, v_p, batch, seq, rheads)
        y_s, st_s = _rwkv_scan_sample(r_s, w_s, k_s, a_s, b_s, v_s, state_rwkv[l], n_seq, t_new, rheads)
        cat = lambda p, s: jnp.concatenate([p, s], axis=0)
        yb = _rwkv_post(cat(y_p, y_s), cat(bonus_p, bonus_s), cat(g_p, g_s),
                        lnx_g[l].reshape(1, rw), lnx_b[l].reshape(1, rw), e, et, tb=256)

        gblk = g_off // 512
        (merged,) = _matmul(
            yb, p_b[l].astype(BF16), tm=1024, tn=512,
            epilogue=lambda acc, ya, ga, gb: (jax.nn.sigmoid(ga) * ya + jax.nn.sigmoid(gb) * acc,),
            out_dtypes=(BF16,), extras=((y_a, 0), (z, gblk), (z, gblk + d_model // 512)))
        (pre,) = _matmul(merged, w_out[l].astype(BF16), tm=1024, tn=512,
                         epilogue=lambda acc, xr: (alpha * xr + acc,), out_dtypes=(F32,), extras=((x, 0),))
        x, xb = _layer_norm(pre, ln2_g[l], ln2_b[l], tr=256)

        x, xb = _swiglu_block(x, xb, w_ffn2_in[l], w_ffn2_out[l].astype(BF16),
                              ln3_g[l], ln3_b[l], alpha)

        kv = lambda lo, n0, n1, shape: z[n0:n1, lo:lo + att_w].reshape(shape)
        outs_p.append((kv(att_w, 0, n_p, (batch, seq, heads, ATT_V_DIM)),
                       kv(2 * att_w, 0, n_p, (batch, seq, heads, ATT_V_DIM)),
                       st_p, z[seq - 1:n_p:seq, sh_off:sh_off + shift_w]))
        outs_s.append((kv(att_w, n_p, n_tok, (n_seq, t_new, heads, ATT_V_DIM)),
                       kv(2 * att_w, n_p, n_tok, (n_seq, t_new, heads, ATT_V_DIM)),
                       st_s, z[n_p + t_new - 1:n_tok:t_new, sh_off:sh_off + shift_w]))

    k_p, v_p, s_p, sh_p = (jnp.stack(t) for t in zip(*outs_p))
    k_s, v_s, s_s, sh_s = (jnp.stack(t) for t in zip(*outs_s))
    return (x[:n_p].reshape(batch, seq, d_model), x[n_p:].reshape(n_seq, t_new, d_model),
            k_p, v_p, s_p, sh_p, k_s, v_s, s_s, sh_s)
```

```python
import functools
import math

import jax
import jax.numpy as jnp
from jax import lax
from jax.experimental import pallas as pl
from jax.experimental.pallas import tpu as pltpu

F32 = jnp.float32
BF16 = jnp.bfloat16

LANES = 128
SUBLANES = 8
MXU_DIM = 256
VMEM_LIMIT_BYTES = 56 * 1024 * 1024

ATT_HEAD_DIM = 128
ATT_V_DIM = 2 * ATT_HEAD_DIM
ATT_SCALE = ATT_HEAD_DIM ** -0.5
NEG_INF = -1e30
RMS_EPS = 1e-5
RWKV_HEAD = 64
GN_EPS = 64e-5
LN_EPS = 1e-5


def _round_up(x, m):
    return (x + m - 1) // m * m


def _params(*semantics):
    return pltpu.CompilerParams(dimension_semantics=semantics,
                                vmem_limit_bytes=VMEM_LIMIT_BYTES)


def _mm_kernel(*refs, extra_splits, epilogue):
    x_ref, w_ref = refs[0], refs[1]
    pos, tiles = 2, []
    for n_first in extra_splits:
        if n_first is None:
            tiles.append(refs[pos][...])
            pos += 1
        else:
            tiles.append(jnp.where(pl.program_id(0) < n_first, refs[pos][...], refs[pos + 1][...]))
            pos += 2
    outs = refs[pos:]
    acc = jnp.dot(x_ref[...].astype(BF16), w_ref[...], preferred_element_type=F32)
    res = epilogue(acc, *tiles)
    for o_ref, r in zip(outs, res):
        o_ref[...] = r.astype(o_ref.dtype)


def _matmul(x, w, *, tm, tn, epilogue, out_dtypes, extras=(), n_cols=None):
    m, k = x.shape
    n = w.shape[1] if n_cols is None else n_cols
    assert m % tm == 0 and n % tn == 0 and w.shape[0] == k
    in_specs = [pl.BlockSpec((tm, k), lambda i, j: (i, 0)),
                pl.BlockSpec((k, tn), lambda i, j: (0, j))]
    args = [x, w]
    extra_splits = []
    for src, off in extras:
        if isinstance(src, tuple):
            first, second = src
            nf = first.shape[0] // tm
            assert first.shape[0] % tm == 0 and second.shape[0] % tm == 0
            assert first.shape[0] + second.shape[0] == m
            in_specs.append(pl.BlockSpec((tm, tn), lambda i, j, off=off, nf=nf: (jnp.minimum(i, nf - 1), j + off)))
            in_specs.append(pl.BlockSpec((tm, tn), lambda i, j, off=off, nf=nf: (jnp.maximum(i - nf, 0), j + off)))
            args += [first, second]
            extra_splits.append(nf)
        else:
            in_specs.append(pl.BlockSpec((tm, tn), lambda i, j, off=off: (i, j + off)))
            args.append(src)
            extra_splits.append(None)
    return pl.pallas_call(
        functools.partial(_mm_kernel, extra_splits=tuple(extra_splits), epilogue=epilogue),
        grid=(m // tm, n // tn),
        in_specs=in_specs,
        out_specs=[pl.BlockSpec((tm, tn), lambda i, j: (i, j)) for _ in out_dtypes],
        out_shape=[jax.ShapeDtypeStruct((m, n), dt) for dt in out_dtypes],
        compiler_params=_params("parallel", "arbitrary"),
    )(*args)


def _ffn_in_kernel(x_ref, wg_ref, wu_ref, h_ref):
    x = x_ref[...]
    g = jnp.dot(x, wg_ref[...].astype(BF16), preferred_element_type=F32)
    u = jnp.dot(x, wu_ref[...].astype(BF16), preferred_element_type=F32)
    h_ref[...] = (g * jax.nn.sigmoid(g) * u).astype(h_ref.dtype)


def _ffn_in(x, w, *, tm, tn):
    m, d = x.shape
    f = w.shape[1] // 2
    assert m % tm == 0 and f % tn == 0
    nf = f // tn
    return pl.pallas_call(
        _ffn_in_kernel,
        grid=(m // tm, nf),
        in_specs=[pl.BlockSpec((tm, d), lambda i, j: (i, 0)),
                  pl.BlockSpec((d, tn), lambda i, j: (0, j)),
                  pl.BlockSpec((d, tn), lambda i, j: (0, j + nf))],
        out_specs=pl.BlockSpec((tm, tn), lambda i, j: (i, j)),
        out_shape=jax.ShapeDtypeStruct((m, f), BF16),
        compiler_params=_params("parallel", "arbitrary"),
    )(x, w, w)


def _ln_kernel(x_ref, g_ref, b_ref, o_ref, ob_ref):
    x = x_ref[...]
    mu = jnp.mean(x, axis=-1, keepdims=True)
    xc = x - mu
    var = jnp.mean(xc * xc, axis=-1, keepdims=True)
    y = xc * lax.rsqrt(var + LN_EPS) * g_ref[...] + b_ref[...]
    o_ref[...] = y
    ob_ref[...] = y.astype(BF16)


def _layer_norm(x, g, b, *, tr):
    m, d = x.shape
    assert m % tr == 0
    row = pl.BlockSpec((tr, d), lambda i: (i, 0))
    vec = pl.BlockSpec((1, d), lambda i: (0, 0))
    return pl.pallas_call(
        _ln_kernel,
        grid=(m // tr,),
        in_specs=[row, vec, vec],
        out_specs=[row, row],
        out_shape=[jax.ShapeDtypeStruct((m, d), F32), jax.ShapeDtypeStruct((m, d), BF16)],
        compiler_params=_params("parallel"),
    )(x, g.reshape(1, d), b.reshape(1, d))


def _swiglu_block(x, xb, w_in, w_out_b, g, b, alpha):
    h = _ffn_in(xb, w_in, tm=1024, tn=256)
    (pre,) = _matmul(h, w_out_b, tm=512, tn=256,
                     epilogue=lambda acc, xr: (alpha * xr + 0.5 * acc,),
                     out_dtypes=(F32,), extras=((x, 0),))
    return _layer_norm(pre, g, b, tr=256)


EXP2_SCALE = ATT_SCALE * math.log2(math.e)


def _lambda_value(lq1, lk1, lq2, lk2, lam_init):
    return (jnp.exp(jnp.sum(lq1[...] * lk1[...], keepdims=True))
            - jnp.exp(jnp.sum(lq2[...] * lk2[...], keepdims=True)) + lam_init)


def _sub_norm(o, g, lam_init):
    ms = jnp.mean(o * o, axis=-1, keepdims=True)
    return o * lax.rsqrt(ms + RMS_EPS) * g * (1.0 - lam_init)


def _online_softmax_step(s, v, m_ref, l_ref, acc_ref):
    m_prev = m_ref[...]
    m_new = jnp.maximum(m_prev, jnp.max(s, axis=-1, keepdims=True))
    p = jnp.exp2((s - m_new) * EXP2_SCALE)
    corr = jnp.exp2((m_prev - m_new) * EXP2_SCALE)
    l_ref[...] = corr * l_ref[...] + jnp.sum(p, axis=-1, keepdims=True)
    acc_ref[...] = corr * acc_ref[...] + jnp.dot(p.astype(BF16), v, preferred_element_type=F32)
    m_ref[...] = m_new


def _prompt_attn_kernel(lq1, lk1, lq2, lk2, g_ref, q_ref, k_ref, v_ref, o_ref,
                        kb, vb, m1, l1, a1, m2, l2, a2, *, tq, lam_init):
    qi = pl.program_id(2)
    branches = ((0, m1, l1, a1), (ATT_HEAD_DIM, m2, l2, a2))

    @pl.when(qi == 0)
    def _():
        kb[...] = k_ref[...].astype(BF16)
        vb[...] = v_ref[...].astype(BF16)

    for _, m_ref, l_ref, a_ref in branches:
        m_ref[...] = jnp.full(m_ref.shape, NEG_INF, F32)
        l_ref[...] = jnp.zeros(l_ref.shape, F32)
        a_ref[...] = jnp.zeros(a_ref.shape, F32)
    q = q_ref[...].astype(BF16)

    def kv_block(ki, causal):
        start = pl.multiple_of(ki * tq, tq)
        k = kb[pl.ds(start, tq), :]
        v = vb[pl.ds(start, tq), :]
        for lo, m_ref, l_ref, a_ref in branches:
            s = lax.dot_general(q[:, lo:lo + ATT_HEAD_DIM], k[:, lo:lo + ATT_HEAD_DIM],
                                (((1,), (1,)), ((), ())), preferred_element_type=F32)
            if causal:
                rows = lax.broadcasted_iota(jnp.int32, (tq, tq), 0)
                cols = lax.broadcasted_iota(jnp.int32, (tq, tq), 1)
                s = jnp.where(cols <= rows, s, NEG_INF)
            _online_softmax_step(s, v, m_ref, l_ref, a_ref)

    def body(ki, carry):
        kv_block(ki, False)
        return carry

    lax.fori_loop(0, qi, body, 0)
    kv_block(qi, True)

    lam = _lambda_value(lq1, lk1, lq2, lk2, lam_init)
    o = a1[...] / l1[...] - lam * (a2[...] / l2[...])
    o_ref[...] = _sub_norm(o, g_ref[...], lam_init).astype(o_ref.dtype)


def _prompt_attention(z, lam_vecs, subln_g, lam_init, *, batch, seq, heads, tq):
    nq = seq // tq
    kv_off = heads
    vec = pl.BlockSpec((1, ATT_HEAD_DIM), lambda b, h, qi: (0, 0))
    blk = (tq, ATT_V_DIM)
    full = (seq, ATT_V_DIM)
    return pl.pallas_call(
        functools.partial(_prompt_attn_kernel, tq=tq, lam_init=lam_init),
        grid=(batch, heads, nq),
        in_specs=[vec, vec, vec, vec,
                  pl.BlockSpec((1, ATT_V_DIM), lambda b, h, qi: (0, 0)),
                  pl.BlockSpec(blk, lambda b, h, qi: (b * nq + qi, h)),
                  pl.BlockSpec(full, lambda b, h, qi: (b, kv_off + h)),
                  pl.BlockSpec(full, lambda b, h, qi: (b, 2 * kv_off + h))],
        out_specs=pl.BlockSpec(blk, lambda b, h, qi: (b * nq + qi, h)),
        out_shape=jax.ShapeDtypeStruct((batch * seq, heads * ATT_V_DIM), BF16),
        scratch_shapes=[pltpu.VMEM(full, BF16), pltpu.VMEM(full, BF16),
                        pltpu.VMEM((tq, 1), F32), pltpu.VMEM((tq, 1), F32), pltpu.VMEM((tq, ATT_V_DIM), F32),
                        pltpu.VMEM((tq, 1), F32), pltpu.VMEM((tq, 1), F32), pltpu.VMEM((tq, ATT_V_DIM), F32)],
        compiler_params=_params("parallel", "parallel", "arbitrary"),
    )(*lam_vecs, subln_g.reshape(1, ATT_V_DIM), z, z, z)


def _sample_attn_kernel(*refs, n_pages, heads, t_new, page, lam_init):
    pt_ref = refs[0]
    lq1, lk1, lq2, lk2, g_ref, q_ref, kn_ref, vn_ref = refs[1:9]
    k_refs = refs[9:9 + n_pages]
    v_refs = refs[9 + n_pages:9 + 2 * n_pages]
    o_ref = refs[9 + 2 * n_pages]
    qt, kb, vb, m_ref, l_ref, acc_ref = refs[10 + 2 * n_pages:]
    del pt_ref
    j = pl.program_id(1)
    rows = 2 * heads * t_new
    prow = page * heads

    @pl.when(j == 0)
    def _():
        q = q_ref[...]
        lane = lax.broadcasted_iota(jnp.int32, (t_new, ATT_V_DIM), 1)
        pieces = []
        for br in range(2):
            keep = (lane >= ATT_HEAD_DIM) if br else (lane < ATT_HEAD_DIM)
            for h in range(heads):
                pieces.append(jnp.where(keep, q[:, h * ATT_V_DIM:(h + 1) * ATT_V_DIM], 0.0))
        qt[...] = jnp.concatenate(pieces, axis=0).astype(BF16)
        m_ref[...] = jnp.full(m_ref.shape, NEG_INF, F32)
        l_ref[...] = jnp.zeros(l_ref.shape, F32)
        acc_ref[...] = jnp.zeros(acc_ref.shape, F32)

    def scores(keys, n_keys, causal):
        s = lax.dot_general(keys, qt[...], (((1,), (1,)), ((), ())), preferred_element_type=F32)
        s = s.T
        r_idx = lax.broadcasted_iota(jnp.int32, (rows, n_keys * heads), 0)
        c_idx = lax.broadcasted_iota(jnp.int32, (rows, n_keys * heads), 1)
        keep = (c_idx % heads) == ((r_idx // t_new) % heads)
        if causal:
            keep = keep & ((c_idx // heads) <= (r_idx % t_new))
        return jnp.where(keep, s, NEG_INF)

    for i in range(n_pages):
        kb[i * prow:(i + 1) * prow, :] = k_refs[i][...].reshape(prow, ATT_V_DIM).astype(BF16)
        vb[i * prow:(i + 1) * prow, :] = v_refs[i][...].reshape(prow, ATT_V_DIM).astype(BF16)
    _online_softmax_step(scores(kb[...], n_pages * page, False), vb[...], m_ref, l_ref, acc_ref)

    @pl.when(j == pl.num_programs(1) - 1)
    def _():
        def rows_of(ref):
            x = ref[...].reshape(t_new, heads, ATT_V_DIM).reshape(t_new * heads, ATT_V_DIM)
            pad = jnp.zeros((LANES - t_new * heads, ATT_V_DIM), F32)
            return jnp.concatenate([x, pad], axis=0).astype(BF16)
        _online_softmax_step(scores(rows_of(kn_ref), LANES // heads, True), rows_of(vn_ref),
                             m_ref, l_ref, acc_ref)
        lam = _lambda_value(lq1, lk1, lq2, lk2, lam_init)
        o = acc_ref[...] / l_ref[...]
        half = heads * t_new
        for h in range(heads):
            oh = o[h * t_new:(h + 1) * t_new, :] - lam * o[half + h * t_new:half + (h + 1) * t_new, :]
            o_ref[:, h * ATT_V_DIM:(h + 1) * ATT_V_DIM] = _sub_norm(oh, g_ref[...], lam_init)


def _sample_attention(z, row0, cache_k, cache_v, page0, page_table, lam_vecs, subln_g, lam_init,
                      *, heads, t_new, pages_per_step):
    n_seq, n_logical = page_table.shape
    page = cache_k.shape[1]
    width = heads * ATT_V_DIM
    rows = 2 * heads * t_new
    assert rows == LANES and t_new == SUBLANES and n_logical % pages_per_step == 0
    assert row0 % t_new == 0
    nsteps = n_logical // pages_per_step
    rb0 = row0 // t_new
    vec = pl.BlockSpec((1, ATT_HEAD_DIM), lambda b, j, pt: (0, 0))

    def page_spec(i):
        return pl.BlockSpec(
            (None, page, heads, ATT_V_DIM),
            lambda b, j, pt, i=i: (page0 + pt[b * n_logical + j * pages_per_step + i], 0, 0, 0))

    new_spec = lambda c: pl.BlockSpec((t_new, width), lambda b, j, pt, c=c: (rb0 + b, c))
    key_rows = pages_per_step * page * heads
    grid_spec = pltpu.PrefetchScalarGridSpec(
        num_scalar_prefetch=1,
        grid=(n_seq, nsteps),
        in_specs=[vec, vec, vec, vec,
                  pl.BlockSpec((1, ATT_V_DIM), lambda b, j, pt: (0, 0)),
                  new_spec(0), new_spec(1), new_spec(2)]
                 + [page_spec(i) for i in range(pages_per_step)]
                 + [page_spec(i) for i in range(pages_per_step)],
        out_specs=pl.BlockSpec((t_new, width), lambda b, j, pt: (b, 0)),
        scratch_shapes=[pltpu.VMEM((rows, ATT_V_DIM), BF16),
                        pltpu.VMEM((key_rows, ATT_V_DIM), BF16),
                        pltpu.VMEM((key_rows, ATT_V_DIM), BF16),
                        pltpu.VMEM((rows, 1), F32), pltpu.VMEM((rows, 1), F32),
                        pltpu.VMEM((rows, ATT_V_DIM), F32)])
    return pl.pallas_call(
        functools.partial(_sample_attn_kernel, n_pages=pages_per_step, heads=heads,
                          t_new=t_new, page=page, lam_init=lam_init),
        grid_spec=grid_spec,
        out_shape=jax.ShapeDtypeStruct((n_seq * t_new, width), F32),
        compiler_params=_params("parallel", "arbitrary"),
    )(page_table.reshape(-1), *lam_vecs, subln_g.reshape(1, ATT_V_DIM), z, z, z,
      *([cache_k] * pages_per_step), *([cache_v] * pages_per_step))


def _dot_indicator(x, ind):
    x1 = x.astype(BF16)
    r1 = x - x1.astype(F32)
    x2 = r1.astype(BF16)
    x3 = (r1 - x2.astype(F32)).astype(BF16)
    return (jnp.dot(x1, ind, preferred_element_type=F32) + jnp.dot(x2, ind, preferred_element_type=F32)
            + jnp.dot(x3, ind, preferred_element_type=F32))


def _head_sum(x, e_ref, et_ref):
    return _dot_indicator(_dot_indicator(x, e_ref[...]), et_ref[...])


def _rwkv_prep_kernel(zs_ref, prev_ref, init_ref, mu_ref, w0_ref, w2_ref, a0_ref, a2_ref, g2_ref,
                      kk_ref, ka_ref, rk_ref, e_ref, et_ref,
                      r_out, w_out, k_out, v_out, a_out, b_out, g_out, bonus_out,
                      *, tb, seq_len, width, lora_w, lora_a):
    zs = zs_ref[...]
    row = lax.broadcasted_iota(jnp.int32, (tb, 1), 0)
    shifted = pltpu.roll(zs, 1, 0)
    if seq_len >= tb:
        at_start = (pl.program_id(0) * tb) % seq_len == 0
        carry = jnp.where(at_start, init_ref[0:1, :], prev_ref[SUBLANES - 1:SUBLANES, :])
        z_prev = jnp.where(row == 0, carry, shifted)
    else:
        z_prev = jnp.where(row % seq_len == 0, init_ref[...], shifted)
    zm = zs + (z_prev - zs) * mu_ref[...]
    r = zm[:, 0:width]
    kr = zm[:, width:2 * width]
    vr = zm[:, 2 * width:3 * width]
    o = 3 * width
    wd = zm[:, o:o + lora_w]
    ad = zm[:, o + lora_w:o + lora_w + lora_a]
    gd = zm[:, o + lora_w + lora_a:]
    wl = w0_ref[...] + jnp.dot(jnp.tanh(wd).astype(BF16), w2_ref[...], preferred_element_type=F32)
    softplus = jnp.maximum(-wl, 0.0) + jnp.log1p(jnp.exp(-jnp.abs(wl)))
    w_log = -softplus - 0.5
    decay = jnp.exp(-jnp.exp(w_log))
    a = jax.nn.sigmoid(a0_ref[...] + jnp.dot(ad.astype(BF16), a2_ref[...], preferred_element_type=F32))
    g = jnp.dot(jax.nn.sigmoid(gd).astype(BF16), g2_ref[...], preferred_element_type=F32)
    kk = kr * kk_ref[...]
    norm = jnp.sqrt(_head_sum(kk * kk, e_ref, et_ref))
    kk = kk / jnp.maximum(norm, 1e-12)
    kmod = kr * (1.0 + (a - 1.0) * ka_ref[...])
    bonus = _head_sum(r * kmod * rk_ref[...], e_ref, et_ref) * vr
    r_out[...] = r
    w_out[...] = decay
    k_out[...] = kmod
    v_out[...] = vr
    a_out[...] = -kk
    b_out[...] = kk * a
    g_out[...] = g
    bonus_out[...] = bonus


def _rwkv_prep(z, col0, zs_cols, row0, n_rows, init_rows, seq_len, lp, e, et, *, tb, width, lora_w, lora_a):
    assert n_rows % tb == 0 and row0 % tb == 0 and col0 % zs_cols == 0 and tb % SUBLANES == 0
    cb = col0 // zs_cols
    rb0 = row0 // tb
    nb = n_rows // tb
    per8 = tb // SUBLANES
    zs_spec = pl.BlockSpec((tb, zs_cols), lambda i: (rb0 + i, cb))
    prev_spec = pl.BlockSpec((SUBLANES, zs_cols),
                             lambda i: (jnp.maximum((rb0 + i) * per8 - 1, 0), cb))
    if seq_len >= tb:
        assert seq_len % tb == 0
        spb = seq_len // tb
        init_spec = pl.BlockSpec((SUBLANES, zs_cols), lambda i: (i // spb, 0))
    else:
        assert tb % seq_len == 0
        init_spec = pl.BlockSpec((tb, zs_cols), lambda i: (i, 0))
    full = lambda arr: pl.BlockSpec(arr.shape, lambda i: (0,) * arr.ndim)
    consts = [lp['mu'], lp['w0'], lp['w2'], lp['a0'], lp['a2'], lp['g2'],
              lp['k_k'], lp['k_a'], lp['r_k'], e, et]
    out_spec = pl.BlockSpec((tb, width), lambda i: (i, 0))
    return pl.pallas_call(
        functools.partial(_rwkv_prep_kernel, tb=tb, seq_len=seq_len, width=width,
                          lora_w=lora_w, lora_a=lora_a),
        grid=(nb,),
        in_specs=[zs_spec, prev_spec, init_spec] + [full(c) for c in consts],
        out_specs=[out_spec] * 8,
        out_shape=[jax.ShapeDtypeStruct((n_rows, width), F32)] * 8,
        compiler_params=_params("parallel"),
    )(z, z, init_rows, *consts)


def _rwkv_scan_kernel(r_ref, w_ref, k_ref, a_ref, an_ref, b_ref, v_ref, s0_ref, y_ref, sout_ref,
                      s_scr, sa_scr, *, tb, n_key, n_val):
    t_blk = pl.program_id(1)
    nv = n_val // SUBLANES
    vrows = [slice(j * SUBLANES, (j + 1) * SUBLANES) for j in range(nv)]

    def bcast(ref, t, c):
        return jnp.broadcast_to(ref[0, t, pl.ds(c, 1), :], (SUBLANES, LANES))

    @pl.when(t_blk == 0)
    def _():
        s_scr[...] = s0_ref[0]
        sa0 = [jnp.zeros((SUBLANES, LANES), F32) for _ in range(nv)]
        for c in range(n_key):
            a_c = bcast(a_ref, 0, c)
            for j in range(nv):
                sa0[j] = sa0[j] + s0_ref[0, c, vrows[j], :] * a_c
        for j in range(nv):
            sa_scr[vrows[j], :] = sa0[j]

    def token(t, sa):
        last = t == tb - 1
        t_next = jnp.minimum(t + 1, tb - 1)
        vt = [v_ref[0, t, vrows[j], :] for j in range(nv)]
        y = [jnp.zeros((SUBLANES, LANES), F32) for _ in range(nv)]
        sa_next = [jnp.zeros((SUBLANES, LANES), F32) for _ in range(nv)]
        for c in range(n_key):
            w_c = bcast(w_ref, t, c)
            b_c = bcast(b_ref, t, c)
            k_c = bcast(k_ref, t, c)
            r_c = bcast(r_ref, t, c)
            a_c = jnp.where(last, bcast(an_ref, 0, c), bcast(a_ref, t_next, c))
            for j in range(nv):
                s_new = s_scr[c, vrows[j], :] * w_c + sa[j] * b_c + vt[j] * k_c
                s_scr[c, vrows[j], :] = s_new
                y[j] = y[j] + s_new * r_c
                sa_next[j] = sa_next[j] + s_new * a_c
        for j in range(nv):
            y_ref[0, t, vrows[j], :] = y[j]
        return tuple(sa_next)

    sa_end = lax.fori_loop(0, tb, token, tuple(sa_scr[vrows[j], :] for j in range(nv)))
    for j in range(nv):
        sa_scr[vrows[j], :] = sa_end[j]

    @pl.when(t_blk == pl.num_programs(1) - 1)
    def _():
        sout_ref[0] = s_scr[...]


def _rwkv_scan(r, w, k, a, b, v, s0, *, tb):
    groups, t_len, n_key, _ = r.shape
    n_val = v.shape[2]
    assert t_len % tb == 0
    nt = t_len // tb
    kspec = pl.BlockSpec((1, tb, n_key, LANES), lambda g, t: (g, t, 0, 0))
    nspec = pl.BlockSpec((1, tb, n_key, LANES), lambda g, t: (g, jnp.minimum(t + 1, nt - 1), 0, 0))
    vspec = pl.BlockSpec((1, tb, n_val, LANES), lambda g, t: (g, t, 0, 0))
    sspec = pl.BlockSpec((1, n_key, n_val, LANES), lambda g, t: (g, 0, 0, 0))
    return pl.pallas_call(
        functools.partial(_rwkv_scan_kernel, tb=tb, n_key=n_key, n_val=n_val),
        grid=(groups, nt),
        in_specs=[kspec] * 4 + [nspec, kspec, vspec, sspec],
        out_specs=[vspec, sspec],
        out_shape=[jax.ShapeDtypeStruct(v.shape, F32), jax.ShapeDtypeStruct(s0.shape, F32)],
        scratch_shapes=[pltpu.VMEM((n_key, n_val, LANES), F32), pltpu.VMEM((n_val, LANES), F32)],
        compiler_params=_params("parallel", "arbitrary"),
    )(r, w, k, a, a, b, v, s0)


def _rwkv_post_kernel(y_ref, bonus_ref, g_ref, lg_ref, lb_ref, e_ref, et_ref, o_ref, *, head):
    y = y_ref[...]
    mu = _head_sum(y, e_ref, et_ref) * (1.0 / head)
    yc = y - mu
    var = _head_sum(yc * yc, e_ref, et_ref) * (1.0 / head)
    yn = yc * lax.rsqrt(var + GN_EPS) * lg_ref[...] + lb_ref[...]
    o_ref[...] = ((yn + bonus_ref[...]) * g_ref[...]).astype(o_ref.dtype)


def _rwkv_post(y, bonus, g, lnx_g, lnx_b, e, et, *, tb):
    n, width = y.shape
    assert n % tb == 0
    row = pl.BlockSpec((tb, width), lambda i: (i, 0))
    full = lambda arr: pl.BlockSpec(arr.shape, lambda i: (0,) * arr.ndim)
    return pl.pallas_call(
        functools.partial(_rwkv_post_kernel, head=RWKV_HEAD),
        grid=(n // tb,),
        in_specs=[row, row, row, full(lnx_g), full(lnx_b), full(e), full(et)],
        out_specs=row,
        out_shape=jax.ShapeDtypeStruct((n, width), BF16),
        compiler_params=_params("parallel"),
    )(y, bonus, g, lnx_g, lnx_b, e, et)


def _to_lanes_kernel(*refs, batch, heads, split, tb, n_key):
    n_in = (n_key + 1) * batch
    ins, outs = refs[:n_in], refs[n_in:]
    vl = RWKV_HEAD // split

    def per_head(ref):
        return ref[...].reshape(tb, heads, RWKV_HEAD)

    for i in range(n_key):
        parts = []
        for b in range(batch):
            parts += [per_head(ins[i * batch + b])] * split
        outs[i][...] = jnp.swapaxes(jnp.concatenate(parts, axis=1), 1, 2)
    parts = []
    for b in range(batch):
        x = per_head(ins[n_key * batch + b])
        parts += [x[:, :, s * vl:(s + 1) * vl] for s in range(split)]
    outs[n_key][...] = jnp.swapaxes(jnp.concatenate(parts, axis=1), 1, 2)


def _to_lanes(key_arrays, v, *, batch, seq, heads, split, tb):
    width = heads * RWKV_HEAD
    n_key = len(key_arrays)
    nblk = seq // tb
    assert seq % tb == 0 and batch * split * heads == LANES
    vl = RWKV_HEAD // split
    in_specs, args = [], []
    for arr in list(key_arrays) + [v]:
        for b in range(batch):
            in_specs.append(pl.BlockSpec((tb, width), lambda i, b=b: (b * nblk + i, 0)))
            args.append(arr)
    kout = pl.BlockSpec((tb, RWKV_HEAD, LANES), lambda i: (i, 0, 0))
    vout = pl.BlockSpec((tb, vl, LANES), lambda i: (i, 0, 0))
    return pl.pallas_call(
        functools.partial(_to_lanes_kernel, batch=batch, heads=heads, split=split, tb=tb, n_key=n_key),
        grid=(nblk,),
        in_specs=in_specs,
        out_specs=[kout] * n_key + [vout],
        out_shape=[jax.ShapeDtypeStruct((seq, RWKV_HEAD, LANES), F32)] * n_key
                  + [jax.ShapeDtypeStruct((seq, vl, LANES), F32)],
        compiler_params=_params("parallel"),
    )(*args)


def _from_lanes_kernel(y_ref, o_ref, *, batch, heads, split, tb):
    yt = jnp.swapaxes(y_ref[...], 1, 2)
    for b in range(batch):
        pieces = [yt[:, (b * split + s) * heads:(b * split + s + 1) * heads, :] for s in range(split)]
        o_ref[b] = jnp.concatenate(pieces, axis=2).reshape(tb, heads * RWKV_HEAD)


def _from_lanes(y, *, batch, seq, heads, split, tb):
    vl = RWKV_HEAD // split
    return pl.pallas_call(
        functools.partial(_from_lanes_kernel, batch=batch, heads=heads, split=split, tb=tb),
        grid=(seq // tb,),
        in_specs=[pl.BlockSpec((tb, vl, LANES), lambda i: (i, 0, 0))],
        out_specs=pl.BlockSpec((batch, tb, heads * RWKV_HEAD), lambda i: (0, i, 0)),
        out_shape=jax.ShapeDtypeStruct((batch, seq, heads * RWKV_HEAD), F32),
        compiler_params=_params("parallel"),
    )(y)


def _rwkv_scan_prompt(r, w, k, a, b, v, batch, seq, heads):
    split = LANES // (batch * heads)
    assert split >= 1 and batch * heads * split == LANES and RWKV_HEAD % (split * SUBLANES) == 0
    vl = RWKV_HEAD // split
    geom = dict(batch=batch, seq=seq, heads=heads, split=split, tb=32)
    rl, wl, kl, al, bl, vv = (t[None] for t in _to_lanes((r, w, k, a, b), v, **geom))
    s0 = jnp.zeros((1, RWKV_HEAD, vl, LANES), F32)
    y, s = _rwkv_scan(rl, wl, kl, al, bl, vv, s0, tb=32)
    y = _from_lanes(y[0], **geom).reshape(batch * seq, heads * RWKV_HEAD)
    s = s.reshape(RWKV_HEAD, vl, batch, split, heads).transpose(2, 4, 3, 1, 0)
    return y, s.reshape(batch, heads, RWKV_HEAD, RWKV_HEAD)


def _rwkv_scan_sample(r, w, k, a, b, v, s0, n_seq, t_new, heads):
    assert n_seq == LANES
    to = lambda t: t.reshape(n_seq, t_new, heads, RWKV_HEAD).transpose(2, 1, 3, 0)
    s_l = s0.transpose(1, 3, 2, 0)
    y, s = _rwkv_scan(to(r), to(w), to(k), to(a), to(b), to(v), s_l, tb=t_new)
    y = y.transpose(3, 1, 0, 2).reshape(n_seq * t_new, heads * RWKV_HEAD)
    return y, s.transpose(3, 0, 2, 1)


def kernel(x_prompt, x_sample, cache_k, cache_v, state_rwkv, state_shift, page_table, w_ffn1_in, w_ffn1_out, ln1_g, ln1_b, w_in, lambda_q1, lambda_k1, lambda_q2, lambda_k2, subln_g, p_a, mu_shift, w0, w2, a0, a2, g2, k_k, k_a, r_k, lnx_g, lnx_b, p_b, w_out, ln2_g, ln2_b, w_ffn2_in, w_ffn2_out, ln3_g, ln3_b):
    batch, seq, d_model = x_prompt.shape
    n_seq, t_new, _ = x_sample.shape
    depth = w_in.shape[0]
    heads = cache_k.shape[3]
    att_w = heads * ATT_V_DIM
    rw = p_b.shape[1]
    rheads = rw // RWKV_HEAD
    lora_w, lora_a, lora_g = w2.shape[1], a2.shape[1], g2.shape[1]
    shift_w = state_shift.shape[2]
    assert shift_w == 3 * rw + lora_w + lora_a + lora_g and rw == att_w
    alpha = (2 * depth) ** 0.25
    n_p, n_s = batch * seq, n_seq * t_new
    n_tok = n_p + n_s

    col_blk = 1024
    sh_pad = _round_up(shift_w, col_blk)
    g_off = 3 * att_w
    sh_off = g_off + 2 * d_model
    assert sh_off % sh_pad == 0
    lora_g_pad = sh_pad - (3 * rw + lora_w + lora_a)
    src_sh, src_g = 3 * att_w, 3 * att_w + shift_w

    ch = jnp.arange(rw) // RWKV_HEAD
    e = (ch[:, None] == jnp.arange(LANES)[None, :]).astype(BF16)
    et = e.T

    n_pool = cache_k.shape[1]
    ck = cache_k.reshape((depth * n_pool,) + cache_k.shape[2:])
    cv = cache_v.reshape((depth * n_pool,) + cache_v.shape[2:])
    x = (x_prompt.reshape(n_p, d_model), x_sample.reshape(n_s, d_model))
    xb = jnp.concatenate(x, axis=0).astype(BF16)
    outs_p, outs_s = [], []
    for l in range(depth):
        lam_init = 0.8 - 0.6 * math.exp(-0.3 * l)
        w_in_l = w_in[l]
        w_in_p = jnp.concatenate(
            [w_in_l[:, :src_sh], w_in_l[:, src_g:], w_in_l[:, src_sh:src_g],
             jnp.zeros((d_model, sh_pad - shift_w), F32)], axis=1).astype(BF16)
        lam_vecs = [t[l].reshape(1, ATT_HEAD_DIM) for t in (lambda_q1, lambda_k1, lambda_q2, lambda_k2)]
        pad_cols = lambda t: jnp.pad(t, ((0, 0), (0, sh_pad - shift_w)))
        lp = dict(mu=pad_cols(mu_shift[l].reshape(1, shift_w)), w0=w0[l].reshape(1, rw), w2=w2[l].astype(BF16),
                  a0=a0[l].reshape(1, rw), a2=a2[l].astype(BF16),
                  g2=jnp.pad(g2[l], ((0, lora_g_pad - lora_g), (0, 0))).astype(BF16),
                  k_k=k_k[l].reshape(1, rw), k_a=k_a[l].reshape(1, rw), r_k=r_k[l].reshape(1, rw))

        x, xb = _swiglu_block(x, xb, w_ffn1_in[l], w_ffn1_out[l].astype(BF16),
                              ln1_g[l], ln1_b[l], alpha)

        (z,) = _matmul(xb, w_in_p, tm=1024, tn=512, epilogue=lambda acc: (acc,), out_dtypes=(F32,))

        o_p = _prompt_attention(z, lam_vecs, subln_g[l], lam_init, batch=batch, seq=seq, heads=heads, tq=1024)
        o_s = _sample_attention(z, n_p, ck, cv, l * n_pool, page_table, lam_vecs, subln_g[l], lam_init,
                                heads=heads, t_new=t_new, pages_per_step=8)
        o = jnp.concatenate([o_p, o_s.astype(BF16)], axis=0)
        (y_a,) = _matmul(o, p_a[l].astype(BF16), tm=1024, tn=512, epilogue=lambda acc: (acc,), out_dtypes=(F32,))

        prep = functools.partial(_rwkv_prep, z, sh_off, sh_pad, lp=lp, e=e, et=et,
                                 width=rw, lora_w=lora_w, lora_a=lora_a)
        init_p = jnp.zeros((batch * SUBLANES, sh_pad), F32)
        init_s = jnp.pad(state_shift[l][:, None, :],
                         ((0, 0), (0, t_new - 1), (0, sh_pad - shift_w))).reshape(n_s, sh_pad)
        r_p, w_p, k_p, v_p, a_p, b_p, g_p, bonus_p = prep(0, n_p, init_p, seq, tb=128)
        r_s, w_s, k_s, v_s, a_s, b_s, g_s, bonus_s = prep(n_p, n_s, init_s, t_new, tb=128)
        y_p, st_p = _rwkv_scan_prompt(r_p, w_p, k_p, a_p, b_p, v_p, batch, seq, rheads)
        y_s, st_s = _rwkv_scan_sample(r_s, w_s, k_s, a_s, b_s, v_s, state_rwkv[l], n_seq, t_new, rheads)
        post = functools.partial(_rwkv_post, lnx_g=lnx_g[l].reshape(1, rw), lnx_b=lnx_b[l].reshape(1, rw),
                                 e=e, et=et, tb=256)
        yb = jnp.concatenate([post(y_p, bonus_p, g_p), post(y_s, bonus_s, g_s)], axis=0)

        gblk = g_off // 512
        (merged,) = _matmul(
            yb, p_b[l].astype(BF16), tm=1024, tn=512,
            epilogue=lambda acc, ya, ga, gb: (jax.nn.sigmoid(ga) * ya + jax.nn.sigmoid(gb) * acc,),
            out_dtypes=(BF16,), extras=((y_a, 0), (z, gblk), (z, gblk + d_model // 512)))
        (pre,) = _matmul(merged, w_out[l].astype(BF16), tm=1024, tn=512,
                         epilogue=lambda acc, xr: (alpha * xr + acc,), out_dtypes=(F32,), extras=((x, 0),))
        x, xb = _layer_norm(pre, ln2_g[l], ln2_b[l], tr=256)

        x, xb = _swiglu_block(x, xb, w_ffn2_in[l], w_ffn2_out[l].astype(BF16),
                              ln3_g[l], ln3_b[l], alpha)

        kv = lambda lo, n0, n1, shape: z[n0:n1, lo:lo + att_w].reshape(shape)
        outs_p.append((kv(att_w, 0, n_p, (batch, seq, heads, ATT_V_DIM)),
                       kv(2 * att_w, 0, n_p, (batch, seq, heads, ATT_V_DIM)),
                       st_p, z[seq - 1:n_p:seq, sh_off:sh_off + shift_w]))
        outs_s.append((kv(att_w, n_p, n_tok, (n_seq, t_new, heads, ATT_V_DIM)),
                       kv(2 * att_w, n_p, n_tok, (n_seq, t_new, heads, ATT_V_DIM)),
                       st_s, z[n_p + t_new - 1:n_tok:t_new, sh_off:sh_off + shift_w]))

    k_p, v_p, s_p, sh_p = (jnp.stack(t) for t in zip(*outs_p))
    k_s, v_s, s_s, sh_s = (jnp.stack(t) for t in zip(*outs_s))
    return (x[:n_p].reshape(batch, seq, d_model), x[n_p:].reshape(n_seq, t_new, d_model),
            k_p, v_p, s_p, sh_p, k_s, v_s, s_s, sh_s)
```

```python
import functools
import math

import jax
import jax.numpy as jnp
from jax import lax
from jax.experimental import pallas as pl
from jax.experimental.pallas import tpu as pltpu

F32 = jnp.float32
BF16 = jnp.bfloat16

LANES = 128
SUBLANES = 8
MXU_DIM = 256
VMEM_LIMIT_BYTES = 56 * 1024 * 1024

ATT_HEAD_DIM = 128
ATT_V_DIM = 2 * ATT_HEAD_DIM
ATT_SCALE = ATT_HEAD_DIM ** -0.5
NEG_INF = -1e30
RMS_EPS = 1e-5
RWKV_HEAD = 64
GN_EPS = 64e-5
LN_EPS = 1e-5


def _round_up(x, m):
    return (x + m - 1) // m * m


def _params(*semantics):
    return pltpu.CompilerParams(dimension_semantics=semantics,
                                vmem_limit_bytes=VMEM_LIMIT_BYTES)


def _mm_kernel(*refs, extra_splits, epilogue):
    x_ref, w_ref = refs[0], refs[1]
    pos, tiles = 2, []
    for n_first in extra_splits:
        if n_first is None:
            tiles.append(refs[pos][...])
            pos += 1
        else:
            tiles.append(jnp.where(pl.program_id(0) < n_first, refs[pos][...], refs[pos + 1][...]))
            pos += 2
    outs = refs[pos:]
    acc = jnp.dot(x_ref[...].astype(BF16), w_ref[...], preferred_element_type=F32)
    res = epilogue(acc, *tiles)
    for o_ref, r in zip(outs, res):
        o_ref[...] = r.astype(o_ref.dtype)


def _matmul(x, w, *, tm, tn, epilogue, out_dtypes, extras=(), n_cols=None):
    m, k = x.shape
    n = w.shape[1] if n_cols is None else n_cols
    assert m % tm == 0 and n % tn == 0 and w.shape[0] == k
    in_specs = [pl.BlockSpec((tm, k), lambda i, j: (i, 0)),
                pl.BlockSpec((k, tn), lambda i, j: (0, j))]
    args = [x, w]
    extra_splits = []
    for src, off in extras:
        if isinstance(src, tuple):
            first, second = src
            nf = first.shape[0] // tm
            assert first.shape[0] % tm == 0 and second.shape[0] % tm == 0
            assert first.shape[0] + second.shape[0] == m
            in_specs.append(pl.BlockSpec((tm, tn), lambda i, j, off=off, nf=nf: (jnp.minimum(i, nf - 1), j + off)))
            in_specs.append(pl.BlockSpec((tm, tn), lambda i, j, off=off, nf=nf: (jnp.maximum(i - nf, 0), j + off)))
            args += [first, second]
            extra_splits.append(nf)
        else:
            in_specs.append(pl.BlockSpec((tm, tn), lambda i, j, off=off: (i, j + off)))
            args.append(src)
            extra_splits.append(None)
    return pl.pallas_call(
        functools.partial(_mm_kernel, extra_splits=tuple(extra_splits), epilogue=epilogue),
        grid=(m // tm, n // tn),
        in_specs=in_specs,
        out_specs=[pl.BlockSpec((tm, tn), lambda i, j: (i, j)) for _ in out_dtypes],
        out_shape=[jax.ShapeDtypeStruct((m, n), dt) for dt in out_dtypes],
        compiler_params=_params("parallel", "arbitrary"),
    )(*args)


def _ffn_in_kernel(x_ref, wg_ref, wu_ref, h_ref):
    x = x_ref[...]
    g = jnp.dot(x, wg_ref[...].astype(BF16), preferred_element_type=F32)
    u = jnp.dot(x, wu_ref[...].astype(BF16), preferred_element_type=F32)
    h_ref[...] = (g * jax.nn.sigmoid(g) * u).astype(h_ref.dtype)


def _ffn_in(x, w, *, tm, tn):
    m, d = x.shape
    f = w.shape[1] // 2
    assert m % tm == 0 and f % tn == 0
    nf = f // tn
    return pl.pallas_call(
        _ffn_in_kernel,
        grid=(m // tm, nf),
        in_specs=[pl.BlockSpec((tm, d), lambda i, j: (i, 0)),
                  pl.BlockSpec((d, tn), lambda i, j: (0, j)),
                  pl.BlockSpec((d, tn), lambda i, j: (0, j + nf))],
        out_specs=pl.BlockSpec((tm, tn), lambda i, j: (i, j)),
        out_shape=jax.ShapeDtypeStruct((m, f), BF16),
        compiler_params=_params("parallel", "arbitrary"),
    )(x, w, w)


def _ln_kernel(x_ref, g_ref, b_ref, o_ref, ob_ref):
    x = x_ref[...]
    mu = jnp.mean(x, axis=-1, keepdims=True)
    xc = x - mu
    var = jnp.mean(xc * xc, axis=-1, keepdims=True)
    y = xc * lax.rsqrt(var + LN_EPS) * g_ref[...] + b_ref[...]
    o_ref[...] = y
    ob_ref[...] = y.astype(BF16)


def _layer_norm(x, g, b, *, tr):
    m, d = x.shape
    assert m % tr == 0
    row = pl.BlockSpec((tr, d), lambda i: (i, 0))
    vec = pl.BlockSpec((1, d), lambda i: (0, 0))
    return pl.pallas_call(
        _ln_kernel,
        grid=(m // tr,),
        in_specs=[row, vec, vec],
        out_specs=[row, row],
        out_shape=[jax.ShapeDtypeStruct((m, d), F32), jax.ShapeDtypeStruct((m, d), BF16)],
        compiler_params=_params("parallel"),
    )(x, g.reshape(1, d), b.reshape(1, d))


def _swiglu_block(x, xb, w_in, w_out_b, g, b, alpha):
    h = _ffn_in(xb, w_in, tm=1024, tn=256)
    (pre,) = _matmul(h, w_out_b, tm=512, tn=256,
                     epilogue=lambda acc, xr: (alpha * xr + 0.5 * acc,),
                     out_dtypes=(F32,), extras=((x, 0),))
    return _layer_norm(pre, g, b, tr=256)


EXP2_SCALE = ATT_SCALE * math.log2(math.e)


def _lambda_value(lq1, lk1, lq2, lk2, lam_init):
    return (jnp.exp(jnp.sum(lq1[...] * lk1[...], keepdims=True))
            - jnp.exp(jnp.sum(lq2[...] * lk2[...], keepdims=True)) + lam_init)


def _sub_norm(o, g, lam_init):
    ms = jnp.mean(o * o, axis=-1, keepdims=True)
    return o * lax.rsqrt(ms + RMS_EPS) * g * (1.0 - lam_init)


def _online_softmax_step(s, v, m_ref, l_ref, acc_ref):
    m_prev = m_ref[...]
    m_new = jnp.maximum(m_prev, jnp.max(s, axis=-1, keepdims=True))
    p = jnp.exp2((s - m_new) * EXP2_SCALE)
    corr = jnp.exp2((m_prev - m_new) * EXP2_SCALE)
    l_ref[...] = corr * l_ref[...] + jnp.sum(p, axis=-1, keepdims=True)
    acc_ref[...] = corr * acc_ref[...] + jnp.dot(p.astype(BF16), v, preferred_element_type=F32)
    m_ref[...] = m_new


def _prompt_attn_kernel(lq1, lk1, lq2, lk2, g_ref, q_ref, k_ref, v_ref, o_ref,
                        kb, vb, m1, l1, a1, m2, l2, a2, *, tq, lam_init):
    qi = pl.program_id(2)
    branches = ((0, m1, l1, a1), (ATT_HEAD_DIM, m2, l2, a2))

    @pl.when(qi == 0)
    def _():
        kb[...] = k_ref[...].astype(BF16)
        vb[...] = v_ref[...].astype(BF16)

    for _, m_ref, l_ref, a_ref in branches:
        m_ref[...] = jnp.full(m_ref.shape, NEG_INF, F32)
        l_ref[...] = jnp.zeros(l_ref.shape, F32)
        a_ref[...] = jnp.zeros(a_ref.shape, F32)
    q = q_ref[...].astype(BF16)

    def kv_block(ki, causal):
        start = pl.multiple_of(ki * tq, tq)
        k = kb[pl.ds(start, tq), :]
        v = vb[pl.ds(start, tq), :]
        for lo, m_ref, l_ref, a_ref in branches:
            s = lax.dot_general(q[:, lo:lo + ATT_HEAD_DIM], k[:, lo:lo + ATT_HEAD_DIM],
                                (((1,), (1,)), ((), ())), preferred_element_type=F32)
            if causal:
                rows = lax.broadcasted_iota(jnp.int32, (tq, tq), 0)
                cols = lax.broadcasted_iota(jnp.int32, (tq, tq), 1)
                s = jnp.where(cols <= rows, s, NEG_INF)
            _online_softmax_step(s, v, m_ref, l_ref, a_ref)

    def body(ki, carry):
        kv_block(ki, False)
        return carry

    lax.fori_loop(0, qi, body, 0)
    kv_block(qi, True)

    lam = _lambda_value(lq1, lk1, lq2, lk2, lam_init)
    o = a1[...] / l1[...] - lam * (a2[...] / l2[...])
    o_ref[...] = _sub_norm(o, g_ref[...], lam_init).astype(o_ref.dtype)


def _prompt_attention(z, lam_vecs, subln_g, lam_init, *, batch, seq, heads, tq):
    nq = seq // tq
    kv_off = heads
    vec = pl.BlockSpec((1, ATT_HEAD_DIM), lambda b, h, qi: (0, 0))
    blk = (tq, ATT_V_DIM)
    full = (seq, ATT_V_DIM)
    return pl.pallas_call(
        functools.partial(_prompt_attn_kernel, tq=tq, lam_init=lam_init),
        grid=(batch, heads, nq),
        in_specs=[vec, vec, vec, vec,
                  pl.BlockSpec((1, ATT_V_DIM), lambda b, h, qi: (0, 0)),
                  pl.BlockSpec(blk, lambda b, h, qi: (b * nq + qi, h)),
                  pl.BlockSpec(full, lambda b, h, qi: (b, kv_off + h)),
                  pl.BlockSpec(full, lambda b, h, qi: (b, 2 * kv_off + h))],
        out_specs=pl.BlockSpec(blk, lambda b, h, qi: (b * nq + qi, h)),
        out_shape=jax.ShapeDtypeStruct((batch * seq, heads * ATT_V_DIM), BF16),
        scratch_shapes=[pltpu.VMEM(full, BF16), pltpu.VMEM(full, BF16),
                        pltpu.VMEM((tq, 1), F32), pltpu.VMEM((tq, 1), F32), pltpu.VMEM((tq, ATT_V_DIM), F32),
                        pltpu.VMEM((tq, 1), F32), pltpu.VMEM((tq, 1), F32), pltpu.VMEM((tq, ATT_V_DIM), F32)],
        compiler_params=_params("parallel", "parallel", "arbitrary"),
    )(*lam_vecs, subln_g.reshape(1, ATT_V_DIM), z, z, z)


def _sample_attn_kernel(*refs, n_pages, heads, t_new, page, lam_init):
    pt_ref = refs[0]
    lq1, lk1, lq2, lk2, g_ref, q_ref, kn_ref, vn_ref = refs[1:9]
    k_refs = refs[9:9 + n_pages]
    v_refs = refs[9 + n_pages:9 + 2 * n_pages]
    o_ref = refs[9 + 2 * n_pages]
    qt, kb, vb, m_ref, l_ref, acc_ref = refs[10 + 2 * n_pages:]
    del pt_ref
    j = pl.program_id(1)
    rows = 2 * heads * t_new
    prow = page * heads

    @pl.when(j == 0)
    def _():
        q = q_ref[...]
        lane = lax.broadcasted_iota(jnp.int32, (t_new, ATT_V_DIM), 1)
        pieces = []
        for br in range(2):
            keep = (lane >= ATT_HEAD_DIM) if br else (lane < ATT_HEAD_DIM)
            for h in range(heads):
                pieces.append(jnp.where(keep, q[:, h * ATT_V_DIM:(h + 1) * ATT_V_DIM], 0.0))
        qt[...] = jnp.concatenate(pieces, axis=0).astype(BF16)
        m_ref[...] = jnp.full(m_ref.shape, NEG_INF, F32)
        l_ref[...] = jnp.zeros(l_ref.shape, F32)
        acc_ref[...] = jnp.zeros(acc_ref.shape, F32)

    def scores(keys, n_keys, causal):
        s = lax.dot_general(keys, qt[...], (((1,), (1,)), ((), ())), preferred_element_type=F32)
        s = s.T
        r_idx = lax.broadcasted_iota(jnp.int32, (rows, n_keys * heads), 0)
        c_idx = lax.broadcasted_iota(jnp.int32, (rows, n_keys * heads), 1)
        keep = (c_idx % heads) == ((r_idx // t_new) % heads)
        if causal:
            keep = keep & ((c_idx // heads) <= (r_idx % t_new))
        return jnp.where(keep, s, NEG_INF)

    for i in range(n_pages):
        kb[i * prow:(i + 1) * prow, :] = k_refs[i][...].reshape(prow, ATT_V_DIM).astype(BF16)
        vb[i * prow:(i + 1) * prow, :] = v_refs[i][...].reshape(prow, ATT_V_DIM).astype(BF16)
    _online_softmax_step(scores(kb[...], n_pages * page, False), vb[...], m_ref, l_ref, acc_ref)

    @pl.when(j == pl.num_programs(1) - 1)
    def _():
        def rows_of(ref):
            x = ref[...].reshape(t_new, heads, ATT_V_DIM).reshape(t_new * heads, ATT_V_DIM)
            pad = jnp.zeros((LANES - t_new * heads, ATT_V_DIM), F32)
            return jnp.concatenate([x, pad], axis=0).astype(BF16)
        _online_softmax_step(scores(rows_of(kn_ref), LANES // heads, True), rows_of(vn_ref),
                             m_ref, l_ref, acc_ref)
        lam = _lambda_value(lq1, lk1, lq2, lk2, lam_init)
        o = acc_ref[...] / l_ref[...]
        half = heads * t_new
        for h in range(heads):
            oh = o[h * t_new:(h + 1) * t_new, :] - lam * o[half + h * t_new:half + (h + 1) * t_new, :]
            o_ref[:, h * ATT_V_DIM:(h + 1) * ATT_V_DIM] = _sub_norm(oh, g_ref[...], lam_init)


def _sample_attention(z, row0, cache_k, cache_v, page0, page_table, lam_vecs, subln_g, lam_init,
                      *, heads, t_new, pages_per_step):
    n_seq, n_logical = page_table.shape
    page = cache_k.shape[1]
    width = heads * ATT_V_DIM
    rows = 2 * heads * t_new
    assert rows == LANES and t_new == SUBLANES and n_logical % pages_per_step == 0
    assert row0 % t_new == 0
    nsteps = n_logical // pages_per_step
    rb0 = row0 // t_new
    vec = pl.BlockSpec((1, ATT_HEAD_DIM), lambda b, j, pt: (0, 0))

    def page_spec(i):
        return pl.BlockSpec(
            (None, page, heads, ATT_V_DIM),
            lambda b, j, pt, i=i: (page0 + pt[b * n_logical + j * pages_per_step + i], 0, 0, 0))

    new_spec = lambda c: pl.BlockSpec((t_new, width), lambda b, j, pt, c=c: (rb0 + b, c))
    key_rows = pages_per_step * page * heads
    grid_spec = pltpu.PrefetchScalarGridSpec(
        num_scalar_prefetch=1,
        grid=(n_seq, nsteps),
        in_specs=[vec, vec, vec, vec,
                  pl.BlockSpec((1, ATT_V_DIM), lambda b, j, pt: (0, 0)),
                  new_spec(0), new_spec(1), new_spec(2)]
                 + [page_spec(i) for i in range(pages_per_step)]
                 + [page_spec(i) for i in range(pages_per_step)],
        out_specs=pl.BlockSpec((t_new, width), lambda b, j, pt: (b, 0)),
        scratch_shapes=[pltpu.VMEM((rows, ATT_V_DIM), BF16),
                        pltpu.VMEM((key_rows, ATT_V_DIM), BF16),
                        pltpu.VMEM((key_rows, ATT_V_DIM), BF16),
                        pltpu.VMEM((rows, 1), F32), pltpu.VMEM((rows, 1), F32),
                        pltpu.VMEM((rows, ATT_V_DIM), F32)])
    return pl.pallas_call(
        functools.partial(_sample_attn_kernel, n_pages=pages_per_step, heads=heads,
                          t_new=t_new, page=page, lam_init=lam_init),
        grid_spec=grid_spec,
        out_shape=jax.ShapeDtypeStruct((n_seq * t_new, width), F32),
        compiler_params=_params("parallel", "arbitrary"),
    )(page_table.reshape(-1), *lam_vecs, subln_g.reshape(1, ATT_V_DIM), z, z, z,
      *([cache_k] * pages_per_step), *([cache_v] * pages_per_step))


def _dot_indicator(x, ind):
    x1 = x.astype(BF16)
    r1 = x - x1.astype(F32)
    x2 = r1.astype(BF16)
    x3 = (r1 - x2.astype(F32)).astype(BF16)
    return (jnp.dot(x1, ind, preferred_element_type=F32) + jnp.dot(x2, ind, preferred_element_type=F32)
            + jnp.dot(x3, ind, preferred_element_type=F32))


def _head_sum(x, e_ref, et_ref):
    return _dot_indicator(_dot_indicator(x, e_ref[...]), et_ref[...])


def _rwkv_prep_kernel(zs_ref, prev_ref, init_ref, mu_ref, w0_ref, w2_ref, a0_ref, a2_ref, g2_ref,
                      kk_ref, ka_ref, rk_ref, e_ref, et_ref,
                      r_out, w_out, k_out, v_out, a_out, b_out, g_out, bonus_out,
                      *, tb, seq_len, width, lora_w, lora_a):
    zs = zs_ref[...]
    row = lax.broadcasted_iota(jnp.int32, (tb, 1), 0)
    shifted = pltpu.roll(zs, 1, 0)
    if seq_len >= tb:
        at_start = (pl.program_id(0) * tb) % seq_len == 0
        carry = jnp.where(at_start, init_ref[0:1, :], prev_ref[SUBLANES - 1:SUBLANES, :])
        z_prev = jnp.where(row == 0, carry, shifted)
    else:
        z_prev = jnp.where(row % seq_len == 0, init_ref[...], shifted)
    zm = zs + (z_prev - zs) * mu_ref[...]
    r = zm[:, 0:width]
    kr = zm[:, width:2 * width]
    vr = zm[:, 2 * width:3 * width]
    o = 3 * width
    wd = zm[:, o:o + lora_w]
    ad = zm[:, o + lora_w:o + lora_w + lora_a]
    gd = zm[:, o + lora_w + lora_a:]
    wl = w0_ref[...] + jnp.dot(jnp.tanh(wd).astype(BF16), w2_ref[...], preferred_element_type=F32)
    softplus = jnp.maximum(-wl, 0.0) + jnp.log1p(jnp.exp(-jnp.abs(wl)))
    w_log = -softplus - 0.5
    decay = jnp.exp(-jnp.exp(w_log))
    a = jax.nn.sigmoid(a0_ref[...] + jnp.dot(ad.astype(BF16), a2_ref[...], preferred_element_type=F32))
    g = jnp.dot(jax.nn.sigmoid(gd).astype(BF16), g2_ref[...], preferred_element_type=F32)
    kk = kr * kk_ref[...]
    norm = jnp.sqrt(_head_sum(kk * kk, e_ref, et_ref))
    kk = kk / jnp.maximum(norm, 1e-12)
    kmod = kr * (1.0 + (a - 1.0) * ka_ref[...])
    bonus = _head_sum(r * kmod * rk_ref[...], e_ref, et_ref) * vr
    r_out[...] = r
    w_out[...] = decay
    k_out[...] = kmod
    v_out[...] = vr
    a_out[...] = -kk
    b_out[...] = kk * a
    g_out[...] = g
    bonus_out[...] = bonus


def _rwkv_prep(z, col0, zs_cols, row0, n_rows, init_rows, seq_len, lp, e, et, *, tb, width, lora_w, lora_a):
    assert n_rows % tb == 0 and row0 % tb == 0 and col0 % zs_cols == 0 and tb % SUBLANES == 0
    cb = col0 // zs_cols
    rb0 = row0 // tb
    nb = n_rows // tb
    per8 = tb // SUBLANES
    zs_spec = pl.BlockSpec((tb, zs_cols), lambda i: (rb0 + i, cb))
    prev_spec = pl.BlockSpec((SUBLANES, zs_cols),
                             lambda i: (jnp.maximum((rb0 + i) * per8 - 1, 0), cb))
    if seq_len >= tb:
        assert seq_len % tb == 0
        spb = seq_len // tb
        init_spec = pl.BlockSpec((SUBLANES, zs_cols), lambda i: (i // spb, 0))
    else:
        assert tb % seq_len == 0
        init_spec = pl.BlockSpec((tb, zs_cols), lambda i: (i, 0))
    full = lambda arr: pl.BlockSpec(arr.shape, lambda i: (0,) * arr.ndim)
    consts = [lp['mu'], lp['w0'], lp['w2'], lp['a0'], lp['a2'], lp['g2'],
              lp['k_k'], lp['k_a'], lp['r_k'], e, et]
    out_spec = pl.BlockSpec((tb, width), lambda i: (i, 0))
    return pl.pallas_call(
        functools.partial(_rwkv_prep_kernel, tb=tb, seq_len=seq_len, width=width,
                          lora_w=lora_w, lora_a=lora_a),
        grid=(nb,),
        in_specs=[zs_spec, prev_spec, init_spec] + [full(c) for c in consts],
        out_specs=[out_spec] * 8,
        out_shape=[jax.ShapeDtypeStruct((n_rows, width), F32)] * 8,
        compiler_params=_params("parallel"),
    )(z, z, init_rows, *consts)


def _rwkv_scan_kernel(r_ref, w_ref, k_ref, a_ref, an_ref, b_ref, v_ref, s0_ref, y_ref, sout_ref,
                      s_scr, sa_scr, *, tb, n_key, n_val):
    t_blk = pl.program_id(1)
    nv = n_val // SUBLANES
    vrows = [slice(j * SUBLANES, (j + 1) * SUBLANES) for j in range(nv)]

    def bcast(ref, t, c):
        return jnp.broadcast_to(ref[0, t, pl.ds(c, 1), :], (SUBLANES, LANES))

    @pl.when(t_blk == 0)
    def _():
        s_scr[...] = s0_ref[0]
        sa0 = [jnp.zeros((SUBLANES, LANES), F32) for _ in range(nv)]
        for c in range(n_key):
            a_c = bcast(a_ref, 0, c)
            for j in range(nv):
                sa0[j] = sa0[j] + s0_ref[0, c, vrows[j], :] * a_c
        for j in range(nv):
            sa_scr[vrows[j], :] = sa0[j]

    def token(t, sa):
        last = t == tb - 1
        t_next = jnp.minimum(t + 1, tb - 1)
        vt = [v_ref[0, t, vrows[j], :] for j in range(nv)]
        y = [jnp.zeros((SUBLANES, LANES), F32) for _ in range(nv)]
        sa_next = [jnp.zeros((SUBLANES, LANES), F32) for _ in range(nv)]
        for c in range(n_key):
            w_c = bcast(w_ref, t, c)
            b_c = bcast(b_ref, t, c)
            k_c = bcast(k_ref, t, c)
            r_c = bcast(r_ref, t, c)
            a_c = jnp.where(last, bcast(an_ref, 0, c), bcast(a_ref, t_next, c))
            for j in range(nv):
                s_new = s_scr[c, vrows[j], :] * w_c + sa[j] * b_c + vt[j] * k_c
                s_scr[c, vrows[j], :] = s_new
                y[j] = y[j] + s_new * r_c
                sa_next[j] = sa_next[j] + s_new * a_c
        for j in range(nv):
            y_ref[0, t, vrows[j], :] = y[j]
        return tuple(sa_next)

    sa_end = lax.fori_loop(0, tb, token, tuple(sa_scr[vrows[j], :] for j in range(nv)))
    for j in range(nv):
        sa_scr[vrows[j], :] = sa_end[j]

    @pl.when(t_blk == pl.num_programs(1) - 1)
    def _():
        sout_ref[0] = s_scr[...]


def _rwkv_scan(r, w, k, a, b, v, s0, *, tb):
    groups, t_len, n_key, _ = r.shape
    n_val = v.shape[2]
    assert t_len % tb == 0
    nt = t_len // tb
    kspec = pl.BlockSpec((1, tb, n_key, LANES), lambda g, t: (g, t, 0, 0))
    nspec = pl.BlockSpec((1, tb, n_key, LANES), lambda g, t: (g, jnp.minimum(t + 1, nt - 1), 0, 0))
    vspec = pl.BlockSpec((1, tb, n_val, LANES), lambda g, t: (g, t, 0, 0))
    sspec = pl.BlockSpec((1, n_key, n_val, LANES), lambda g, t: (g, 0, 0, 0))
    return pl.pallas_call(
        functools.partial(_rwkv_scan_kernel, tb=tb, n_key=n_key, n_val=n_val),
        grid=(groups, nt),
        in_specs=[kspec] * 4 + [nspec, kspec, vspec, sspec],
        out_specs=[vspec, sspec],
        out_shape=[jax.ShapeDtypeStruct(v.shape, F32), jax.ShapeDtypeStruct(s0.shape, F32)],
        scratch_shapes=[pltpu.VMEM((n_key, n_val, LANES), F32), pltpu.VMEM((n_val, LANES), F32)],
        compiler_params=_params("parallel", "arbitrary"),
    )(r, w, k, a, a, b, v, s0)


def _rwkv_post_kernel(y_ref, bonus_ref, g_ref, lg_ref, lb_ref, e_ref, et_ref, o_ref, *, head):
    y = y_ref[...]
    mu = _head_sum(y, e_ref, et_ref) * (1.0 / head)
    yc = y - mu
    var = _head_sum(yc * yc, e_ref, et_ref) * (1.0 / head)
    yn = yc * lax.rsqrt(var + GN_EPS) * lg_ref[...] + lb_ref[...]
    o_ref[...] = ((yn + bonus_ref[...]) * g_ref[...]).astype(o_ref.dtype)


def _rwkv_post(y, bonus, g, lnx_g, lnx_b, e, et, *, tb):
    n, width = y.shape
    assert n % tb == 0
    row = pl.BlockSpec((tb, width), lambda i: (i, 0))
    full = lambda arr: pl.BlockSpec(arr.shape, lambda i: (0,) * arr.ndim)
    return pl.pallas_call(
        functools.partial(_rwkv_post_kernel, head=RWKV_HEAD),
        grid=(n // tb,),
        in_specs=[row, row, row, full(lnx_g), full(lnx_b), full(e), full(et)],
        out_specs=row,
        out_shape=jax.ShapeDtypeStruct((n, width), BF16),
        compiler_params=_params("parallel"),
    )(y, bonus, g, lnx_g, lnx_b, e, et)


def _to_lanes_kernel(*refs, batch, heads, split, tb, n_key):
    n_in = (n_key + 1) * batch
    ins, outs = refs[:n_in], refs[n_in:]
    vl = RWKV_HEAD // split

    def per_head(ref):
        return ref[...].reshape(tb, heads, RWKV_HEAD)

    for i in range(n_key):
        parts = []
        for b in range(batch):
            parts += [per_head(ins[i * batch + b])] * split
        outs[i][...] = jnp.swapaxes(jnp.concatenate(parts, axis=1), 1, 2)
    parts = []
    for b in range(batch):
        x = per_head(ins[n_key * batch + b])
        parts += [x[:, :, s * vl:(s + 1) * vl] for s in range(split)]
    outs[n_key][...] = jnp.swapaxes(jnp.concatenate(parts, axis=1), 1, 2)


def _to_lanes(key_arrays, v, *, batch, seq, heads, split, tb):
    width = heads * RWKV_HEAD
    n_key = len(key_arrays)
    nblk = seq // tb
    assert seq % tb == 0 and batch * split * heads == LANES
    vl = RWKV_HEAD // split
    in_specs, args = [], []
    for arr in list(key_arrays) + [v]:
        for b in range(batch):
            in_specs.append(pl.BlockSpec((tb, width), lambda i, b=b: (b * nblk + i, 0)))
            args.append(arr)
    kout = pl.BlockSpec((tb, RWKV_HEAD, LANES), lambda i: (i, 0, 0))
    vout = pl.BlockSpec((tb, vl, LANES), lambda i: (i, 0, 0))
    return pl.pallas_call(
        functools.partial(_to_lanes_kernel, batch=batch, heads=heads, split=split, tb=tb, n_key=n_key),
        grid=(nblk,),
        in_specs=in_specs,
        out_specs=[kout] * n_key + [vout],
        out_shape=[jax.ShapeDtypeStruct((seq, RWKV_HEAD, LANES), F32)] * n_key
                  + [jax.ShapeDtypeStruct((seq, vl, LANES), F32)],
        compiler_params=_params("parallel"),
    )(*args)


def _from_lanes_kernel(y_ref, o_ref, *, batch, heads, split, tb):
    yt = jnp.swapaxes(y_ref[...], 1, 2)
    for b in range(batch):
        pieces = [yt[:, (b * split + s) * heads:(b * split + s + 1) * heads, :] for s in range(split)]
        o_ref[b] = jnp.concatenate(pieces, axis=2).reshape(tb, heads * RWKV_HEAD)


def _from_lanes(y, *, batch, seq, heads, split, tb):
    vl = RWKV_HEAD // split
    return pl.pallas_call(
        functools.partial(_from_lanes_kernel, batch=batch, heads=heads, split=split, tb=tb),
        grid=(seq // tb,),
        in_specs=[pl.BlockSpec((tb, vl, LANES), lambda i: (i, 0, 0))],
        out_specs=pl.BlockSpec((batch, tb, heads * RWKV_HEAD), lambda i: (0, i, 0)),
        out_shape=jax.ShapeDtypeStruct((batch, seq, heads * RWKV_HEAD), F32),
        compiler_params=_params("parallel"),
    )(y)


def _rwkv_scan_prompt(r, w, k, a, b, v, batch, seq, heads):
    split = LANES // (batch * heads)
    assert split >= 1 and batch * heads * split == LANES and RWKV_HEAD % (split * SUBLANES) == 0
    vl = RWKV_HEAD // split
    geom = dict(batch=batch, seq=seq, heads=heads, split=split, tb=32)
    rl, wl, kl, al, bl, vv = (t[None] for t in _to_lanes((r, w, k, a, b), v, **geom))
    s0 = jnp.zeros((1, RWKV_HEAD, vl, LANES), F32)
    y, s = _rwkv_scan(rl, wl, kl, al, bl, vv, s0, tb=32)
    y = _from_lanes(y[0], **geom).reshape(batch * seq, heads * RWKV_HEAD)
    s = s.reshape(RWKV_HEAD, vl, batch, split, heads).transpose(2, 4, 3, 1, 0)
    return y, s.reshape(batch, heads, RWKV_HEAD, RWKV_HEAD)


def _rwkv_scan_sample(r, w, k, a, b, v, s0, n_seq, t_new, heads):
    assert n_seq == LANES
    to = lambda t: t.reshape(n_seq, t_new, heads, RWKV_HEAD).transpose(2, 1, 3, 0)
    s_l = s0.transpose(1, 3, 2, 0)
    y, s = _rwkv_scan(to(r), to(w), to(k), to(a), to(b), to(v), s_l, tb=t_new)
    y = y.transpose(3, 1, 0, 2).reshape(n_seq * t_new, heads * RWKV_HEAD)
    return y, s.transpose(3, 0, 2, 1)


def kernel(x_prompt, x_sample, cache_k, cache_v, state_rwkv, state_shift, page_table, w_ffn1_in, w_ffn1_out, ln1_g, ln1_b, w_in, lambda_q1, lambda_k1, lambda_q2, lambda_k2, subln_g, p_a, mu_shift, w0, w2, a0, a2, g2, k_k, k_a, r_k, lnx_g, lnx_b, p_b, w_out, ln2_g, ln2_b, w_ffn2_in, w_ffn2_out, ln3_g, ln3_b):
    batch, seq, d_model = x_prompt.shape
    n_seq, t_new, _ = x_sample.shape
    depth = w_in.shape[0]
    heads = cache_k.shape[3]
    att_w = heads * ATT_V_DIM
    rw = p_b.shape[1]
    rheads = rw // RWKV_HEAD
    lora_w, lora_a, lora_g = w2.shape[1], a2.shape[1], g2.shape[1]
    shift_w = state_shift.shape[2]
    assert shift_w == 3 * rw + lora_w + lora_a + lora_g and rw == att_w
    alpha = (2 * depth) ** 0.25
    n_p, n_s = batch * seq, n_seq * t_new
    n_tok = n_p + n_s

    col_blk = 1024
    sh_pad = _round_up(shift_w, col_blk)
    g_off = 3 * att_w
    sh_off = g_off + 2 * d_model
    assert sh_off % sh_pad == 0
    lora_g_pad = sh_pad - (3 * rw + lora_w + lora_a)
    src_sh, src_g = 3 * att_w, 3 * att_w + shift_w

    ch = jnp.arange(rw) // RWKV_HEAD
    e = (ch[:, None] == jnp.arange(LANES)[None, :]).astype(BF16)
    et = e.T

    n_pool = cache_k.shape[1]
    ck = cache_k.reshape((depth * n_pool,) + cache_k.shape[2:])
    cv = cache_v.reshape((depth * n_pool,) + cache_v.shape[2:])
    x = (x_prompt.reshape(n_p, d_model), x_sample.reshape(n_s, d_model))
    xb = jnp.concatenate([t.astype(BF16) for t in x], axis=0)
    outs_p, outs_s = [], []
    for l in range(depth):
        lam_init = 0.8 - 0.6 * math.exp(-0.3 * l)
        w_in_l = w_in[l]
        w_in_p = jnp.concatenate(
            [w_in_l[:, :src_sh], w_in_l[:, src_g:], w_in_l[:, src_sh:src_g],
             jnp.zeros((d_model, sh_pad - shift_w), F32)], axis=1).astype(BF16)
        lam_vecs = [t[l].reshape(1, ATT_HEAD_DIM) for t in (lambda_q1, lambda_k1, lambda_q2, lambda_k2)]
        pad_cols = lambda t: jnp.pad(t, ((0, 0), (0, sh_pad - shift_w)))
        lp = dict(mu=pad_cols(mu_shift[l].reshape(1, shift_w)), w0=w0[l].reshape(1, rw), w2=w2[l].astype(BF16),
                  a0=a0[l].reshape(1, rw), a2=a2[l].astype(BF16),
                  g2=jnp.pad(g2[l], ((0, lora_g_pad - lora_g), (0, 0))).astype(BF16),
                  k_k=k_k[l].reshape(1, rw), k_a=k_a[l].reshape(1, rw), r_k=r_k[l].reshape(1, rw))

        x, xb = _swiglu_block(x, xb, w_ffn1_in[l], w_ffn1_out[l].astype(BF16),
                              ln1_g[l], ln1_b[l], alpha)

        (z,) = _matmul(xb, w_in_p, tm=1024, tn=512, epilogue=lambda acc: (acc,), out_dtypes=(F32,))

        o_p = _prompt_attention(z, lam_vecs, subln_g[l], lam_init, batch=batch, seq=seq, heads=heads, tq=1024)
        o_s = _sample_attention(z, n_p, ck, cv, l * n_pool, page_table, lam_vecs, subln_g[l], lam_init,
                                heads=heads, t_new=t_new, pages_per_step=8)
        o = jnp.concatenate([o_p, o_s.astype(BF16)], axis=0)
        (y_a,) = _matmul(o, p_a[l].astype(BF16), tm=1024, tn=512, epilogue=lambda acc: (acc,), out_dtypes=(F32,))

        prep = functools.partial(_rwkv_prep, z, sh_off, sh_pad, lp=lp, e=e, et=et,
                                 width=rw, lora_w=lora_w, lora_a=lora_a)
        init_p = jnp.zeros((batch * SUBLANES, sh_pad), F32)
        init_s = jnp.pad(state_shift[l][:, None, :],
                         ((0, 0), (0, t_new - 1), (0, sh_pad - shift_w))).reshape(n_s, sh_pad)
        r_p, w_p, k_p, v_p, a_p, b_p, g_p, bonus_p = prep(0, n_p, init_p, seq, tb=128)
        r_s, w_s, k_s, v_s, a_s, b_s, g_s, bonus_s = prep(n_p, n_s, init_s, t_new, tb=128)
        y_p, st_p = _rwkv_scan_prompt(r_p, w_p, k_p, a_p, b_p, v_p, batch, seq, rheads)
        y_s, st_s = _rwkv_scan_sample(r_s, w_s, k_s, a_s, b_s, v_s, state_rwkv[l], n_seq, t_new, rheads)
        post = functools.partial(_rwkv_post, lnx_g=lnx_g[l].reshape(1, rw), lnx_b=lnx_b[l].reshape(1, rw),
                                 e=e, et=et, tb=256)
        yb = jnp.concatenate([post(y_p, bonus_p, g_p), post(y_s, bonus_s, g_s)], axis=0)

        gblk = g_off // 512
        (merged,) = _matmul(
            yb, p_b[l].astype(BF16), tm=1024, tn=512,
            epilogue=lambda acc, ya, ga, gb: (jax.nn.sigmoid(ga) * ya + jax.nn.sigmoid(gb) * acc,),
            out_dtypes=(BF16,), extras=((y_a, 0), (z, gblk), (z, gblk + d_model // 512)))
        (pre,) = _matmul(merged, w_out[l].astype(BF16), tm=1024, tn=512,
                         epilogue=lambda acc, xr: (alpha * xr + acc,), out_dtypes=(F32,), extras=((x, 0),))
        x, xb = _layer_norm(pre, ln2_g[l], ln2_b[l], tr=256)

        x, xb = _swiglu_block(x, xb, w_ffn2_in[l], w_ffn2_out[l].astype(BF16),
                              ln3_g[l], ln3_b[l], alpha)

        kv = lambda lo, n0, n1, shape: z[n0:n1, lo:lo + att_w].reshape(shape)
        outs_p.append((kv(att_w, 0, n_p, (batch, seq, heads, ATT_V_DIM)),
                       kv(2 * att_w, 0, n_p, (batch, seq, heads, ATT_V_DIM)),
                       st_p, z[seq - 1:n_p:seq, sh_off:sh_off + shift_w]))
        outs_s.append((kv(att_w, n_p, n_tok, (n_seq, t_new, heads, ATT_V_DIM)),
                       kv(2 * att_w, n_p, n_tok, (n_seq, t_new, heads, ATT_V_DIM)),
                       st_s, z[n_p + t_new - 1:n_tok:t_new, sh_off:sh_off + shift_w]))

    k_p, v_p, s_p, sh_p = (jnp.stack(t) for t in zip(*outs_p))
    k_s, v_s, s_s, sh_s = (jnp.stack(t) for t in zip(*outs_s))
    return (x[:n_p].reshape(batch, seq, d_model), x[n_p:].reshape(n_seq, t_new, d_model),
            k_p, v_p, s_p, sh_p, k_s, v_s, s_s, sh_s)
```
